```python
import math
import jax, jax.numpy as jnp
from jax import lax
import numpy as np

D_MODEL = 1024
BATCH = 1
SEQ = 16384
DEPTH = 2

HEAD_DIM = 64
MOBA_HEADS = 4
MOBA_BLOCK = 256
MOBA_TOPK = 3
SWA_Q_HEADS = 4
SWA_KV_HEADS = 2
SWA_WINDOW = 128
DIFF_HEADS = 4
DIFF_V_DIM = 2 * HEAD_DIM
A_Q = MOBA_HEADS * HEAD_DIM
B_Q = SWA_Q_HEADS * HEAD_DIM
B_KV = SWA_KV_HEADS * HEAD_DIM
C_QK = DIFF_HEADS * 2 * HEAD_DIM
C_V = DIFF_HEADS * DIFF_V_DIM
N_BRANCH = 3
SPLIT_SIZES = (A_Q, A_Q, A_Q, B_Q, B_KV, B_KV, C_QK, C_QK, C_V, N_BRANCH * D_MODEL)
IN_COLS = A_Q * 3 + B_Q + 2 * B_KV + 2 * C_QK + C_V + N_BRANCH * D_MODEL
D_FF = 2816
FFN_HALF = 0.5
ROPE_THETA = 10000.0
Q_BLOCK = 128
EPS = 1e-6
NEG = -1e30

kernel_name = "hybrid_moba_swa_diff_macaron"


def rmsnorm(x, gain):
    xf = x.astype(jnp.float32)
    y = xf * lax.rsqrt(jnp.mean(xf * xf, axis=-1, keepdims=True) + EPS)
    return (y * gain.astype(jnp.float32)).astype(x.dtype)


def rope_tables(seq):
    pos = jnp.arange(seq, dtype=jnp.float32)
    inv_freq = ROPE_THETA ** (-jnp.arange(0, HEAD_DIM, 2, dtype=jnp.float32) / HEAD_DIM)
    ang = pos[:, None] * inv_freq[None, :]
    return jnp.cos(ang), jnp.sin(ang)


def apply_rope(x, cos, sin):
    xf = x.astype(jnp.float32)
    x1, x2 = jnp.split(xf, 2, axis=-1)
    out = jnp.concatenate([x1 * cos - x2 * sin, x2 * cos + x1 * sin], axis=-1)
    return out.astype(x.dtype)


def to_heads(t, n_heads, dim):
    b, s, _ = t.shape
    return t.reshape(b, s, n_heads, dim).transpose(0, 2, 1, 3)


def from_heads(t):
    b, h, s, d = t.shape
    return t.transpose(0, 2, 1, 3).reshape(b, s, h * d)


def qk_prep(t, gain, cos, sin):
    return apply_rope(rmsnorm(t, gain), cos, sin)


def swiglu(x, w_gate, w_up, w_down):
    return (jax.nn.silu(x @ w_gate) * (x @ w_up)) @ w_down


def moba_attention(q, k, v):
    b, h, s, dh = q.shape
    nb = -(-s // MOBA_BLOCK)
    pad = nb * MOBA_BLOCK - s
    kp = jnp.pad(k, ((0, 0), (0, 0), (0, pad), (0, 0)))
    vp = jnp.pad(v, ((0, 0), (0, 0), (0, pad), (0, 0)))
    kb = kp.reshape(b, h, nb, MOBA_BLOCK, dh)
    vb = vp.reshape(b, h, nb, MOBA_BLOCK, dh)
    kmean = jnp.mean(kb.astype(jnp.float32), axis=3)
    n_sel = min(MOBA_TOPK, nb - 1)
    scale = dh ** -0.5
    bi = jnp.arange(b)[:, None, None, None]
    hi = jnp.arange(h)[None, :, None, None]
    n_chunks = s // Q_BLOCK

    def chunk(c):
        start = c * Q_BLOCK
        qc = lax.dynamic_slice_in_dim(q, start, Q_BLOCK, axis=2)
        tq = start + jnp.arange(Q_BLOCK)
        cur = start // MOBA_BLOCK
        own_k = lax.dynamic_index_in_dim(kb, cur, axis=2, keepdims=False)
        own_v = lax.dynamic_index_in_dim(vb, cur, axis=2, keepdims=False)
        kpos = cur * MOBA_BLOCK + jnp.arange(MOBA_BLOCK)
        s_own = jnp.einsum('bhqd,bhkd->bhqk', qc, own_k, preferred_element_type=jnp.float32) * scale
        s_own = jnp.where(kpos[None, :] <= tq[:, None], s_own, NEG)
        if n_sel > 0:
            gate = jnp.einsum('bhqd,bhnd->bhqn', qc.astype(jnp.float32), kmean)
            past = jnp.arange(nb) < cur
            gate = jnp.where(past, gate, NEG)
            _, gidx = lax.top_k(gate, n_sel)
            valid = past[gidx]
            kg = kb[bi, hi, gidx]
            vg = vb[bi, hi, gidx]
            s_sel = jnp.einsum('bhqd,bhqnkd->bhqnk', qc, kg, preferred_element_type=jnp.float32) * scale
            s_sel = jnp.where(valid[..., None], s_sel, NEG)
            n_keys = n_sel * MOBA_BLOCK
            s_all = jnp.concatenate([s_sel.reshape(b, h, Q_BLOCK, n_keys), s_own], axis=-1)
            p = jax.nn.softmax(s_all, axis=-1).astype(v.dtype)
            p_sel = p[..., :n_keys].reshape(b, h, Q_BLOCK, n_sel, MOBA_BLOCK)
            p_own = p[..., n_keys:]
            return (jnp.einsum('bhqnk,bhqnkd->bhqd', p_sel, vg)
                    + jnp.einsum('bhqk,bhkd->bhqd', p_own, own_v))
        p_own = jax.nn.softmax(s_own, axis=-1).astype(v.dtype)
        return jnp.einsum('bhqk,bhkd->bhqd', p_own, own_v)

    outs = lax.map(chunk, jnp.arange(n_chunks))
    return outs.transpose(1, 2, 0, 3, 4).reshape(b, h, s, dh)


def sliding_window_sink_attention(q, k, v, sinks):
    b, hq, s, dh = q.shape
    hkv = k.shape[1]
    g = hq // hkv
    nc = s // Q_BLOCK
    scale = dh ** -0.5
    qb = q.reshape(b, hkv, g, nc, Q_BLOCK, dh)
    kb = k.reshape(b, hkv, nc, Q_BLOCK, dh)
    vb = v.reshape(b, hkv, nc, Q_BLOCK, dh)
    pad = ((0, 0), (0, 0), (1, 0), (0, 0), (0, 0))
    kband = jnp.concatenate([jnp.pad(kb, pad)[:, :, :-1], kb], axis=3)
    vband = jnp.concatenate([jnp.pad(vb, pad)[:, :, :-1], vb], axis=3)
    sc = jnp.einsum('bhgcqd,bhckd->bhgcqk', qb, kband, preferred_element_type=jnp.float32) * scale
    diff = (jnp.arange(Q_BLOCK)[:, None] + Q_BLOCK) - jnp.arange(2 * Q_BLOCK)[None, :]
    in_window = (diff >= 0) & (diff < SWA_WINDOW)
    kabs = (jnp.arange(nc)[:, None] - 1) * Q_BLOCK + jnp.arange(2 * Q_BLOCK)[None, :]
    mask = in_window[None] & (kabs >= 0)[:, None, :]
    sc = jnp.where(mask, sc, NEG)
    sink = sinks.astype(jnp.float32).reshape(hkv, g)[None, :, :, None, None, None]
    sink = jnp.broadcast_to(sink, sc.shape[:-1] + (1,))
    p = jax.nn.softmax(jnp.concatenate([sc, sink], axis=-1), axis=-1)[..., :-1]
    out = jnp.einsum('bhgcqk,bhckd->bhgcqd', p.astype(v.dtype), vband)
    return out.reshape(b, hq, s, dh)


def differential_attention(q1, q2, k1, k2, v, lam):
    b, h, s, dh = q1.shape
    nc = s // Q_BLOCK
    scale = dh ** -0.5
    kpos = jnp.arange(s)

    def chunk(c):
        start = c * Q_BLOCK
        causal = kpos[None, :] <= (start + jnp.arange(Q_BLOCK))[:, None]

        def attn_map(qq, kk):
            qc = lax.dynamic_slice_in_dim(qq, start, Q_BLOCK, axis=2)
            sc = jnp.einsum('bhqd,bhkd->bhqk', qc, kk, preferred_element_type=jnp.float32) * scale
            return jax.nn.softmax(jnp.where(causal, sc, NEG), axis=-1)

        a = attn_map(q1, k1) - lam * attn_map(q2, k2)
        return jnp.einsum('bhqk,bhkd->bhqd', a.astype(v.dtype), v)

    outs = lax.map(chunk, jnp.arange(nc))
    return outs.transpose(1, 2, 0, 3, 4).reshape(b, h, s, v.shape[-1])


def token_mixing(h, cos, sin, lam_init, w_in, moba_q_norm, moba_k_norm, swa_q_norm,
                 swa_k_norm, swa_sinks, diff_q_norm, diff_k_norm, lq1, lk1, lq2, lk2,
                 diff_subln, w_branch_a, w_branch_b, w_branch_c, w_out):
    b, s, _ = h.shape
    points = np.cumsum(SPLIT_SIZES)[:-1].tolist()
    proj = h @ w_in
    qa, ka, va, qb, kb, vb, qc, kc, vc, gate_logits = jnp.split(proj, points, axis=-1)
    qa = qk_prep(to_heads(qa, MOBA_HEADS, HEAD_DIM), moba_q_norm, cos, sin)
    ka = qk_prep(to_heads(ka, MOBA_HEADS, HEAD_DIM), moba_k_norm, cos, sin)
    va = to_heads(va, MOBA_HEADS, HEAD_DIM)
    ya = from_heads(moba_attention(qa, ka, va)) @ w_branch_a
    qb = qk_prep(to_heads(qb, SWA_Q_HEADS, HEAD_DIM), swa_q_norm, cos, sin)
    kb = qk_prep(to_heads(kb, SWA_KV_HEADS, HEAD_DIM), swa_k_norm, cos, sin)
    vb = to_heads(vb, SWA_KV_HEADS, HEAD_DIM)
    yb = from_heads(sliding_window_sink_attention(qb, kb, vb, swa_sinks)) @ w_branch_b
    q1, q2 = jnp.split(to_heads(qc, DIFF_HEADS, 2 * HEAD_DIM), 2, axis=-1)
    k1, k2 = jnp.split(to_heads(kc, DIFF_HEADS, 2 * HEAD_DIM), 2, axis=-1)
    q1 = qk_prep(q1, diff_q_norm, cos, sin)
    q2 = qk_prep(q2, diff_q_norm, cos, sin)
    k1 = qk_prep(k1, diff_k_norm, cos, sin)
    k2 = qk_prep(k2, diff_k_norm, cos, sin)
    vc = to_heads(vc, DIFF_HEADS, DIFF_V_DIM)
    f32 = jnp.float32
    lam = (jnp.exp(jnp.sum(lq1.astype(f32) * lk1.astype(f32)))
           - jnp.exp(jnp.sum(lq2.astype(f32) * lk2.astype(f32))) + lam_init)
    oc = differential_attention(q1, q2, k1, k2, vc, lam)
    oc = rmsnorm(oc, diff_subln) * (1.0 - lam_init)
    yc = from_heads(oc) @ w_branch_c
    gates = jax.nn.sigmoid(gate_logits.astype(f32)).astype(h.dtype).reshape(b, s, N_BRANCH, D_MODEL)
    merged = gates[:, :, 0] * ya + gates[:, :, 1] * yb + gates[:, :, 2] * yc
    return merged @ w_out


def setup_inputs(seed: int = 0) -> dict:
    key = jax.random.key(seed)
    ks = jax.random.split(key, 32)
    L, D, F, dh = DEPTH, D_MODEL, D_FF, HEAD_DIM

    def w(k, shape, fan_in):
        return jax.random.normal(k, shape, jnp.float32) * fan_in ** -0.5

    def gain(k, shape):
        return 1.0 + 0.02 * jax.random.normal(k, shape, jnp.float32)

    def small(k, shape, sc):
        return sc * jax.random.normal(k, shape, jnp.float32)

    return {
        "x": jax.random.normal(ks[0], (BATCH, SEQ, D), jnp.float32),
        "ffn1_norm": gain(ks[1], (L, D)),
        "ffn1_w_gate": w(ks[2], (L, D, F), D),
        "ffn1_w_up": w(ks[3], (L, D, F), D),
        "ffn1_w_down": w(ks[4], (L, F, D), F),
        "mix_norm": gain(ks[5], (L, D)),
        "w_in": w(ks[6], (L, D, IN_COLS), D),
        "moba_q_norm": gain(ks[7], (L, dh)),
        "moba_k_norm": gain(ks[8], (L, dh)),
        "swa_q_norm": gain(ks[9], (L, dh)),
        "swa_k_norm": gain(ks[10], (L, dh)),
        "swa_sinks": small(ks[11], (L, SWA_Q_HEADS), 0.5),
        "diff_q_norm": gain(ks[12], (L, dh)),
        "diff_k_norm": gain(ks[13], (L, dh)),
        "diff_lambda_q1": small(ks[14], (L, dh), 0.1),
        "diff_lambda_k1": small(ks[15], (L, dh), 0.1),
        "diff_lambda_q2": small(ks[16], (L, dh), 0.1),
        "diff_lambda_k2": small(ks[17], (L, dh), 0.1),
        "diff_subln": gain(ks[18], (L, DIFF_V_DIM)),
        "w_branch_a": w(ks[19], (L, A_Q, D), A_Q),
        "w_branch_b": w(ks[20], (L, B_Q, D), B_Q),
        "w_branch_c": w(ks[21], (L, C_V, D), C_V),
        "w_out": w(ks[22], (L, D, D), D),
        "ffn2_norm": gain(ks[23], (L, D)),
        "ffn2_w_gate": w(ks[24], (L, D, F), D),
        "ffn2_w_up": w(ks[25], (L, D, F), D),
        "ffn2_w_down": w(ks[26], (L, F, D), F),
    }


def reference(x, ffn1_norm, ffn1_w_gate, ffn1_w_up, ffn1_w_down, mix_norm, w_in,
              moba_q_norm, moba_k_norm, swa_q_norm, swa_k_norm, swa_sinks,
              diff_q_norm, diff_k_norm, diff_lambda_q1, diff_lambda_k1,
              diff_lambda_q2, diff_lambda_k2, diff_subln, w_branch_a, w_branch_b,
              w_branch_c, w_out, ffn2_norm, ffn2_w_gate, ffn2_w_up, ffn2_w_down):
    cos, sin = rope_tables(x.shape[1])
    for l in range(DEPTH):
        lam_init = 0.8 - 0.6 * math.exp(-0.3 * l)
        x = x + FFN_HALF * swiglu(rmsnorm(x, ffn1_norm[l]), ffn1_w_gate[l], ffn1_w_up[l], ffn1_w_down[l])
        x = x + token_mixing(rmsnorm(x, mix_norm[l]), cos, sin, lam_init, w_in[l],
                             moba_q_norm[l], moba_k_norm[l], swa_q_norm[l], swa_k_norm[l],
                             swa_sinks[l], diff_q_norm[l], diff_k_norm[l],
                             diff_lambda_q1[l], diff_lambda_k1[l], diff_lambda_q2[l],
                             diff_lambda_k2[l], diff_subln[l], w_branch_a[l], w_branch_b[l],
                             w_branch_c[l], w_out[l])
        x = x + FFN_HALF * swiglu(rmsnorm(x, ffn2_norm[l]), ffn2_w_gate[l], ffn2_w_up[l], ffn2_w_down[l])
    return x
```

```python
import functools
import math

import jax
import jax.numpy as jnp
from jax import lax
from jax.experimental import pallas as pl
from jax.experimental.pallas import tpu as pltpu

D_MODEL = 1024
D_FF = 2816
HEAD_DIM = 64
LANES = 128
MOBA_BLOCK = 256
MOBA_TOPK = 3
SWA_WINDOW = 128
N_BRANCH = 3
ROPE_THETA = 10000.0
EPS = 1e-6
NEG = -1e30
FFN_HALF = 0.5

QK_COLS = 2048
V_COLS = 1024
GATE_COLS = N_BRANCH * D_MODEL

ROW_TILE = 512
ATT_TILE = 256
SWA_TILE = 512
VMEM_LIMIT = 48 * 1024 * 1024

f32 = jnp.float32
bf16 = jnp.bfloat16

_NT = (((1,), (1,)), ((), ()))


def _rms(x, gain):
    return x * lax.rsqrt(jnp.mean(x * x, axis=-1, keepdims=True) + EPS) * gain


def _params(n_axes):
    return pltpu.CompilerParams(
        dimension_semantics=("arbitrary",) * n_axes, vmem_limit_bytes=VMEM_LIMIT)


def _const_spec(shape):
    return pl.BlockSpec(shape, lambda *_: (0,) * len(shape), pipeline_mode=pl.Buffered(1))


def _ffn_kernel(x_ref, g_ref, wg_ref, wu_ref, wd_ref, o_ref):
    x = x_ref[...]
    h = _rms(x, g_ref[...]).astype(bf16)
    a = jnp.dot(h, wg_ref[...], preferred_element_type=f32)
    b = jnp.dot(h, wu_ref[...], preferred_element_type=f32)
    act = (a * jax.nn.sigmoid(a) * b).astype(bf16)
    o_ref[...] = x + FFN_HALF * jnp.dot(act, wd_ref[...], preferred_element_type=f32)


def _ffn(x, gain, wg, wu, wd):
    s = x.shape[0]
    row = pl.BlockSpec((ROW_TILE, D_MODEL), lambda i: (i, 0))
    return pl.pallas_call(
        _ffn_kernel,
        grid=(s // ROW_TILE,),
        in_specs=[row, _const_spec((1, D_MODEL)), _const_spec((D_MODEL, D_FF)),
                  _const_spec((D_MODEL, D_FF)), _const_spec((D_FF, D_MODEL))],
        out_specs=row,
        out_shape=jax.ShapeDtypeStruct(x.shape, f32),
        compiler_params=_params(1),
        name="ffn",
    )(x, gain, wg, wu, wd)


def _proj_kernel(x_ref, g_ref, w_ref, hg_ref, cos_ref, sin_ref, qk_ref, v_ref, km_ref):
    x = x_ref[...]
    h = _rms(x, g_ref[...]).astype(bf16)
    proj = jnp.dot(h, w_ref[...], preferred_element_type=f32)
    lane = lax.broadcasted_iota(jnp.int32, (1, LANES), 1)
    first_head = lane < HEAD_DIM
    first_half = (lane % HEAD_DIM) < (HEAD_DIM // 2)
    cos = cos_ref[...]
    sin = sin_ref[...]
    for c in range(QK_COLS // LANES):
        cols = slice(c * LANES, (c + 1) * LANES)
        y = proj[:, cols]
        sq = y * y
        ss0 = jnp.sum(jnp.where(first_head, sq, 0.0), axis=-1, keepdims=True)
        ss1 = jnp.sum(jnp.where(first_head, 0.0, sq), axis=-1, keepdims=True)
        inv = jnp.where(first_head, lax.rsqrt(ss0 / HEAD_DIM + EPS), lax.rsqrt(ss1 / HEAD_DIM + EPS))
        y = y * inv * hg_ref[:, cols]
        partner = jnp.where(first_half, pltpu.roll(y, LANES - HEAD_DIM // 2, 1),
                            pltpu.roll(y, HEAD_DIM // 2, 1))
        y = y * cos + partner * sin
        qk_ref[:, cols] = y.astype(bf16)
        if c in (2, 3):
            for b in range(ROW_TILE // MOBA_BLOCK):
                blk = y[b * MOBA_BLOCK:(b + 1) * MOBA_BLOCK]
                km_ref[b, :, (c - 2) * LANES:(c - 1) * LANES] = jnp.mean(blk, axis=0, keepdims=True)
    v_ref[...] = proj[:, QK_COLS:].astype(bf16)


def _proj(x, gain, w, head_gain, cos, sin):
    s = x.shape[0]
    nb = ROW_TILE // MOBA_BLOCK
    return pl.pallas_call(
        _proj_kernel,
        grid=(s // ROW_TILE,),
        in_specs=[pl.BlockSpec((ROW_TILE, D_MODEL), lambda i: (i, 0)),
                  _const_spec((1, D_MODEL)),
                  _const_spec((D_MODEL, QK_COLS + V_COLS)),
                  _const_spec((1, QK_COLS)),
                  pl.BlockSpec((ROW_TILE, LANES), lambda i: (i, 0)),
                  pl.BlockSpec((ROW_TILE, LANES), lambda i: (i, 0))],
        out_specs=[pl.BlockSpec((ROW_TILE, QK_COLS), lambda i: (i, 0)),
                   pl.BlockSpec((ROW_TILE, V_COLS), lambda i: (i, 0)),
                   pl.BlockSpec((nb, 1, 2 * LANES), lambda i: (i, 0, 0))],
        out_shape=[jax.ShapeDtypeStruct((s, QK_COLS), bf16),
                   jax.ShapeDtypeStruct((s, V_COLS), bf16),
                   jax.ShapeDtypeStruct((s // MOBA_BLOCK, 1, 2 * LANES), f32)],
        compiler_params=_params(1),
        name="qkv_proj",
    )(x, gain, w, head_gain, cos, sin)


def _head_split(q):
    lane = lax.broadcasted_iota(jnp.int32, (1, LANES), 1)
    zero = jnp.zeros_like(q)
    return jnp.where(lane < HEAD_DIM, q, zero), jnp.where(lane < HEAD_DIM, zero, q)


def _flash_init(s, v, m_ref, l_ref, acc_ref, h):
    m = jnp.max(s, axis=-1, keepdims=True)
    p = jnp.exp(s - m)
    m_ref[h] = m
    l_ref[h] = jnp.sum(p, axis=-1, keepdims=True)
    acc_ref[h] = jnp.dot(p.astype(bf16), v, preferred_element_type=f32)


def _flash_step(s, v, m_ref, l_ref, acc_ref, h):
    m_old = m_ref[h]
    m_new = jnp.maximum(m_old, jnp.max(s, axis=-1, keepdims=True))
    alpha = jnp.exp(m_old - m_new)
    p = jnp.exp(s - m_new)
    l_ref[h] = alpha * l_ref[h] + jnp.sum(p, axis=-1, keepdims=True)
    acc_ref[h] = alpha * acc_ref[h] + jnp.dot(p.astype(bf16), v, preferred_element_type=f32)
    m_ref[h] = m_new


def _moba_kernel(q_ref, k_ref, v_ref, km_ref, o_ref, m_ref, l_ref, acc_ref):
    t = pl.program_id(1)
    q_heads = _head_split(q_ref[...])
    n_blocks = km_ref.shape[0]

    km = km_ref[...]
    km_hi = km.astype(bf16)
    rem = km - km_hi.astype(f32)
    km_mid = rem.astype(bf16)
    km_lo = (rem - km_mid.astype(f32)).astype(bf16)
    blk = lax.broadcasted_iota(jnp.int32, (1, n_blocks), 1).astype(f32)
    t_f = t.astype(f32)
    picks = []
    for qh in q_heads:
        gate = (lax.dot_general(qh, km_hi, _NT, preferred_element_type=f32)
                + lax.dot_general(qh, km_mid, _NT, preferred_element_type=f32)
                + lax.dot_general(qh, km_lo, _NT, preferred_element_type=f32))
        gate = jnp.where(blk < t_f, gate, NEG)
        chosen = []
        for _ in range(MOBA_TOPK):
            best = jnp.max(gate, axis=-1, keepdims=True)
            idx = jnp.min(jnp.where(gate == best, blk, 1e9), axis=-1, keepdims=True)
            chosen.append(jnp.where(best > 0.5 * NEG, idx, -1.0))
            gate = jnp.where(blk == idx, 2.0 * NEG, gate)
        picks.append(chosen)

    row = lax.broadcasted_iota(jnp.int32, (ATT_TILE, ATT_TILE), 0)
    col = lax.broadcasted_iota(jnp.int32, (ATT_TILE, ATT_TILE), 1)
    own = pl.ds(pl.multiple_of(t * ATT_TILE, ATT_TILE), ATT_TILE)
    k_own = k_ref[own, :]
    v_own = v_ref[own, :]
    for h, qh in enumerate(q_heads):
        s = lax.dot_general(qh, k_own, _NT, preferred_element_type=f32)
        _flash_init(jnp.where(col <= row, s, NEG), v_own, m_ref, l_ref, acc_ref, h)

    def past_block(j, carry):
        rows = pl.ds(pl.multiple_of(j * MOBA_BLOCK, MOBA_BLOCK), MOBA_BLOCK)
        k_j = k_ref[rows, :]
        v_j = v_ref[rows, :]
        j_f = j.astype(f32)
        for h, qh in enumerate(q_heads):
            s = lax.dot_general(qh, k_j, _NT, preferred_element_type=f32)
            hit = (picks[h][0] == j_f) | (picks[h][1] == j_f) | (picks[h][2] == j_f)
            _flash_step(jnp.where(hit, s, NEG), v_j, m_ref, l_ref, acc_ref, h)
        return carry

    lax.fori_loop(0, t, past_block, 0)

    lane = lax.broadcasted_iota(jnp.int32, (1, LANES), 1)
    out = jnp.where(lane < HEAD_DIM, acc_ref[0] / l_ref[0], acc_ref[1] / l_ref[1])
    o_ref[...] = out.astype(bf16)


def _moba(qk, v, kmean):
    s = qk.shape[0]
    n_blocks = s // MOBA_BLOCK
    return pl.pallas_call(
        _moba_kernel,
        grid=(2, s // ATT_TILE),
        in_specs=[pl.BlockSpec((ATT_TILE, LANES), lambda p, t: (t, p)),
                  pl.BlockSpec((s, LANES), lambda p, t: (0, 2 + p)),
                  pl.BlockSpec((s, LANES), lambda p, t: (0, p)),
                  pl.BlockSpec((n_blocks, LANES), lambda p, t: (0, p))],
        out_specs=pl.BlockSpec((ATT_TILE, LANES), lambda p, t: (t, p)),
        out_shape=jax.ShapeDtypeStruct((s, 2 * LANES), bf16),
        scratch_shapes=[pltpu.VMEM((2, ATT_TILE, 1), f32), pltpu.VMEM((2, ATT_TILE, 1), f32),
                        pltpu.VMEM((2, ATT_TILE, LANES), f32)],
        compiler_params=_params(2),
        name="moba_attention",
    )(qk, qk, v, kmean)


def _swa_kernel(sink_ref, q_ref, k_ref, kp_ref, v_ref, vp_ref, o_ref):
    p_id = pl.program_id(0)
    t = pl.program_id(1)
    sub_tiles = SWA_TILE // SWA_WINDOW
    lane = lax.broadcasted_iota(jnp.int32, (1, LANES), 1)
    qi = lax.broadcasted_iota(jnp.int32, (SWA_WINDOW, 2 * SWA_WINDOW), 0)
    kj = lax.broadcasted_iota(jnp.int32, (SWA_WINDOW, 2 * SWA_WINDOW), 1)
    back = qi + SWA_WINDOW - kj
    in_window = (back >= 0) & (back < SWA_WINDOW)
    for sub in range(sub_tiles):
        rows = slice(sub * SWA_WINDOW, (sub + 1) * SWA_WINDOW)
        if sub == 0:
            k_prev, v_prev = kp_ref[...], vp_ref[...]
            mask = in_window & (kj >= SWA_WINDOW - t * SWA_TILE)
        else:
            prev = slice((sub - 1) * SWA_WINDOW, sub * SWA_WINDOW)
            k_prev, v_prev = k_ref[prev, :], v_ref[prev, :]
            mask = in_window
        k_band = jnp.concatenate([k_prev, k_ref[rows, :]], axis=0)
        v_band = jnp.concatenate([v_prev, v_ref[rows, :]], axis=0)
        outs = []
        for h, qh in enumerate(_head_split(q_ref[rows, :])):
            sink = sink_ref[2 * p_id + h]
            s = lax.dot_general(qh, k_band, _NT, preferred_element_type=f32)
            s = jnp.where(mask, s, NEG)
            m = jnp.maximum(jnp.max(s, axis=-1, keepdims=True), sink)
            p = jnp.exp(s - m)
            denom = jnp.sum(p, axis=-1, keepdims=True) + jnp.exp(sink - m)
            outs.append(jnp.dot(p.astype(bf16), v_band, preferred_element_type=f32) / denom)
        o_ref[rows, :] = jnp.where(lane < HEAD_DIM, outs[0], outs[1]).astype(bf16)


def _swa(qk, v, sinks):
    s = qk.shape[0]
    ratio = SWA_TILE // SWA_WINDOW

    def prev_rows(t):
        return jnp.maximum(t * ratio - 1, 0)

    return pl.pallas_call(
        _swa_kernel,
        grid=(2, s // SWA_TILE),
        in_specs=[pl.BlockSpec(memory_space=pltpu.SMEM),
                  pl.BlockSpec((SWA_TILE, LANES), lambda p, t: (t, 4 + p)),
                  pl.BlockSpec((SWA_TILE, LANES), lambda p, t: (t, 6 + p)),
                  pl.BlockSpec((SWA_WINDOW, LANES), lambda p, t: (prev_rows(t), 6 + p)),
                  pl.BlockSpec((SWA_TILE, LANES), lambda p, t: (t, 2 + p)),
                  pl.BlockSpec((SWA_WINDOW, LANES), lambda p, t: (prev_rows(t), 2 + p))],
        out_specs=pl.BlockSpec((SWA_TILE, LANES), lambda p, t: (t, p)),
        out_shape=jax.ShapeDtypeStruct((s, 2 * LANES), bf16),
        compiler_params=_params(2),
        name="swa_attention",
    )(sinks, qk, qk, qk, v, v)


def _diff_kernel(lam_init, q_ref, k_ref, v_ref, lq1_ref, lk1_ref, lq2_ref, lk2_ref, sub_ref,
                 o_ref, m_ref, l_ref, acc_ref):
    t = pl.program_id(1)
    q_maps = _head_split(q_ref[...])

    row = lax.broadcasted_iota(jnp.int32, (ATT_TILE, ATT_TILE), 0)
    col = lax.broadcasted_iota(jnp.int32, (ATT_TILE, ATT_TILE), 1)
    own = pl.ds(pl.multiple_of(t * ATT_TILE, ATT_TILE), ATT_TILE)
    k_own = k_ref[own, :]
    v_own = v_ref[own, :]
    for h, qh in enumerate(q_maps):
        s = lax.dot_general(qh, k_own, _NT, preferred_element_type=f32)
        _flash_init(jnp.where(col <= row, s, NEG), v_own, m_ref, l_ref, acc_ref, h)

    def past_tile(j, carry):
        rows = pl.ds(pl.multiple_of(j * ATT_TILE, ATT_TILE), ATT_TILE)
        k_j = k_ref[rows, :]
        v_j = v_ref[rows, :]
        for h, qh in enumerate(q_maps):
            s = lax.dot_general(qh, k_j, _NT, preferred_element_type=f32)
            _flash_step(s, v_j, m_ref, l_ref, acc_ref, h)
        return carry

    lax.fori_loop(0, t, past_tile, 0)

    lam = (jnp.exp(jnp.sum(lq1_ref[...] * lk1_ref[...], axis=-1, keepdims=True))
           - jnp.exp(jnp.sum(lq2_ref[...] * lk2_ref[...], axis=-1, keepdims=True)) + lam_init)
    out = acc_ref[0] / l_ref[0] - lam * (acc_ref[1] / l_ref[1])
    out = _rms(out, sub_ref[...]) * (1.0 - lam_init)
    o_ref[...] = out.astype(bf16)


def _diff(qk, v, lq1, lk1, lq2, lk2, subln, lam_init):
    s = qk.shape[0]
    vec = _const_spec((1, HEAD_DIM))
    return pl.pallas_call(
        functools.partial(_diff_kernel, lam_init),
        grid=(4, s // ATT_TILE),
        in_specs=[pl.BlockSpec((ATT_TILE, LANES), lambda h, t: (t, 8 + h)),
                  pl.BlockSpec((s, LANES), lambda h, t: (0, 12 + h)),
                  pl.BlockSpec((s, LANES), lambda h, t: (0, 4 + h)),
                  vec, vec, vec, vec, _const_spec((1, LANES))],
        out_specs=pl.BlockSpec((ATT_TILE, LANES), lambda h, t: (t, h)),
        out_shape=jax.ShapeDtypeStruct((s, 4 * LANES), bf16),
        scratch_shapes=[pltpu.VMEM((2, ATT_TILE, 1), f32), pltpu.VMEM((2, ATT_TILE, 1), f32),
                        pltpu.VMEM((2, ATT_TILE, LANES), f32)],
        compiler_params=_params(2),
        name="diff_attention",
    )(qk, qk, v, lq1, lk1, lq2, lk2, subln)


def _merge_kernel(x_ref, g_ref, wgate_ref, oa_ref, ob_ref, oc_ref, pa_ref, pb_ref, pc_ref, wo_ref, o_ref):
    x = x_ref[...]
    h = _rms(x, g_ref[...]).astype(bf16)
    branches = ((oa_ref, pa_ref), (ob_ref, pb_ref), (oc_ref, pc_ref))
    merged = jnp.zeros_like(x)
    for i, (b_ref, p_ref) in enumerate(branches):
        logits = jnp.dot(h, wgate_ref[:, i * D_MODEL:(i + 1) * D_MODEL], preferred_element_type=f32)
        y = jnp.dot(b_ref[...], p_ref[...], preferred_element_type=f32)
        merged = merged + jax.nn.sigmoid(logits) * y
    o_ref[...] = x + jnp.dot(merged.astype(bf16), wo_ref[...], preferred_element_type=f32)


def _merge(x, gain, w_gate, oa, ob, oc, pa, pb, pc, w_out):
    s = x.shape[0]

    def rows(width):
        return pl.BlockSpec((ROW_TILE, width), lambda i: (i, 0))

    return pl.pallas_call(
        _merge_kernel,
        grid=(s // ROW_TILE,),
        in_specs=[rows(D_MODEL), _const_spec((1, D_MODEL)), _const_spec((D_MODEL, GATE_COLS)),
                  rows(oa.shape[1]), rows(ob.shape[1]), rows(oc.shape[1]),
                  _const_spec(pa.shape), _const_spec(pb.shape), _const_spec(pc.shape),
                  _const_spec((D_MODEL, D_MODEL))],
        out_specs=rows(D_MODEL),
        out_shape=jax.ShapeDtypeStruct(x.shape, f32),
        compiler_params=_params(1),
        name="gated_merge",
    )(x, gain, w_gate, oa, ob, oc, pa, pb, pc, w_out)


def _rope_tables(seq):
    pos = jnp.arange(seq, dtype=f32)
    inv_freq = ROPE_THETA ** (-jnp.arange(0, HEAD_DIM, 2, dtype=f32) / HEAD_DIM)
    ang = pos[:, None] * inv_freq[None, :]
    cos, sin = jnp.cos(ang), jnp.sin(ang)
    return jnp.tile(cos, (1, 4)), jnp.concatenate([-sin, sin, -sin, sin], axis=-1)


def _layer_weights(w_in):
    qa, ka, va = w_in[:, 0:256], w_in[:, 256:512], w_in[:, 512:768]
    qb, kb, vb = w_in[:, 768:1024], w_in[:, 1024:1152], w_in[:, 1152:1280]
    qc, kc, vc = w_in[:, 1280:1792], w_in[:, 1792:2304], w_in[:, 2304:2816]
    gates = w_in[:, 2816:]

    def dup(w):
        h0, h1 = w[:, :HEAD_DIM], w[:, HEAD_DIM:]
        return jnp.concatenate([h0, h0, h1, h1], axis=1)

    w_qkv = jnp.concatenate([qa, ka, qb, dup(kb), qc, kc, va, dup(vb), vc], axis=1)
    return w_qkv.astype(bf16), gates.astype(bf16)


def _head_gains(qa, ka, qb, kb, qc, kc):
    scale = HEAD_DIM ** -0.5
    parts = [jnp.tile(qa * scale, 4), jnp.tile(ka, 4), jnp.tile(qb * scale, 4), jnp.tile(kb, 4),
             jnp.tile(qc * scale, 8), jnp.tile(kc, 8)]
    return jnp.concatenate(parts)[None, :]


def kernel(x, ffn1_norm, ffn1_w_gate, ffn1_w_up, ffn1_w_down, mix_norm, w_in, moba_q_norm, moba_k_norm, swa_q_norm, swa_k_norm, swa_sinks, diff_q_norm, diff_k_norm, diff_lambda_q1, diff_lambda_k1, diff_lambda_q2, diff_lambda_k2, diff_subln, w_branch_a, w_branch_b, w_branch_c, w_out, ffn2_norm, ffn2_w_gate, ffn2_w_up, ffn2_w_down):
    batch, seq, _ = x.shape
    assert batch == 1 and seq % SWA_TILE == 0 and seq % ROW_TILE == 0
    depth = w_in.shape[0]
    cos, sin = _rope_tables(seq)
    xs = x[0]
    for l in range(depth):
        lam_init = 0.8 - 0.6 * math.exp(-0.3 * l)
        xs = _ffn(xs, ffn1_norm[l][None], ffn1_w_gate[l].astype(bf16), ffn1_w_up[l].astype(bf16),
                  ffn1_w_down[l].astype(bf16))
        w_qkv, w_gate = _layer_weights(w_in[l])
        gains = _head_gains(moba_q_norm[l], moba_k_norm[l], swa_q_norm[l], swa_k_norm[l],
                            diff_q_norm[l], diff_k_norm[l])
        qk, v, kmean = _proj(xs, mix_norm[l][None], w_qkv, gains, cos, sin)
        oa = _moba(qk, v, kmean.reshape(seq // MOBA_BLOCK, 2 * LANES))
        ob = _swa(qk, v, swa_sinks[l])
        oc = _diff(qk, v, diff_lambda_q1[l][None], diff_lambda_k1[l][None], diff_lambda_q2[l][None],
                   diff_lambda_k2[l][None], diff_subln[l][None], lam_init)
        xs = _merge(xs, mix_norm[l][None], w_gate, oa, ob, oc, w_branch_a[l].astype(bf16),
                    w_branch_b[l].astype(bf16), w_branch_c[l].astype(bf16), w_out[l].astype(bf16))
        xs = _ffn(xs, ffn2_norm[l][None], ffn2_w_gate[l].astype(bf16), ffn2_w_up[l].astype(bf16),
                  ffn2_w_down[l].astype(bf16))
    return xs[None]
```

```python
import functools
import math

import jax
import jax.numpy as jnp
from jax import lax
from jax.experimental import pallas as pl
from jax.experimental.pallas import tpu as pltpu

D_MODEL = 1024
D_FF = 2816
HEAD_DIM = 64
HALF = HEAD_DIM // 2
LANES = 128
MOBA_BLOCK = 256
MOBA_TOPK = 3
SWA_WINDOW = 128
N_BRANCH = 3
ROPE_THETA = 10000.0
EPS = 1e-6
NEG = -1e30
FFN_HALF = 0.5

NAT_QK_COLS = 1280
NAT_COLS = 1536
NAT_KA, NAT_KC, NAT_QB, NAT_KB, NAT_VB = 0, 2, 6, 8, 10
QT_ROWS = 768
VT_ROWS = 768
GATE_COLS = N_BRANCH * D_MODEL

ROW_TILE = 512
ATT_TILE = 256
GROUP_BLOCKS = 4
KEY_GROUP = GROUP_BLOCKS * MOBA_BLOCK
SWA_TILE = 512
VMEM_LIMIT = 48 * 1024 * 1024

f32 = jnp.float32
bf16 = jnp.bfloat16

_NT = (((1,), (1,)), ((), ()))


def _rms(x, gain):
    return x * lax.rsqrt(jnp.mean(x * x, axis=-1, keepdims=True) + EPS) * gain


def _params(n_axes):
    return pltpu.CompilerParams(
        dimension_semantics=("arbitrary",) * n_axes, vmem_limit_bytes=VMEM_LIMIT)


def _const_spec(shape):
    return pl.BlockSpec(shape, lambda *_: (0,) * len(shape), pipeline_mode=pl.Buffered(1))


def _ffn_kernel(x_ref, g_ref, wg_ref, wu_ref, wd_ref, o_ref):
    x = x_ref[...]
    h = _rms(x, g_ref[...]).astype(bf16)
    a = jnp.dot(h, wg_ref[...], preferred_element_type=f32)
    b = jnp.dot(h, wu_ref[...], preferred_element_type=f32)
    act = (a * jax.nn.sigmoid(a) * b).astype(bf16)
    o_ref[...] = x + FFN_HALF * jnp.dot(act, wd_ref[...], preferred_element_type=f32)


def _ffn(x, gain, wg, wu, wd):
    s = x.shape[0]
    row = pl.BlockSpec((ROW_TILE, D_MODEL), lambda i: (i, 0))
    return pl.pallas_call(
        _ffn_kernel,
        grid=(s // ROW_TILE,),
        in_specs=[row, _const_spec((1, D_MODEL)), _const_spec((D_MODEL, D_FF)),
                  _const_spec((D_MODEL, D_FF)), _const_spec((D_FF, D_MODEL))],
        out_specs=row,
        out_shape=jax.ShapeDtypeStruct(x.shape, f32),
        compiler_params=_params(1),
        name="ffn",
    )(x, gain, wg, wu, wd)


def _proj_kernel(x_ref, g_ref, wn_ref, wt_ref, hg_ref, hgt_ref, cos_ref, sin_ref, cost_ref, sint_ref,
                 nat_ref, qt_ref, vt_ref, km_ref):
    x = x_ref[...]
    h = _rms(x, g_ref[...]).astype(bf16)

    proj = jnp.dot(h, wn_ref[...], preferred_element_type=f32)
    lane = lax.broadcasted_iota(jnp.int32, (1, LANES), 1)
    first_head = lane < HEAD_DIM
    first_half = (lane % HEAD_DIM) < HALF
    cos = cos_ref[...]
    sin = sin_ref[...]
    for c in range(NAT_QK_COLS // LANES):
        cols = slice(c * LANES, (c + 1) * LANES)
        y = proj[:, cols]
        sq = y * y
        ss0 = jnp.sum(jnp.where(first_head, sq, 0.0), axis=-1, keepdims=True)
        ss1 = jnp.sum(jnp.where(first_head, 0.0, sq), axis=-1, keepdims=True)
        inv = jnp.where(first_head, lax.rsqrt(ss0 / HEAD_DIM + EPS), lax.rsqrt(ss1 / HEAD_DIM + EPS))
        y = y * inv * hg_ref[:, cols]
        partner = jnp.where(first_half, pltpu.roll(y, LANES - HALF, 1), pltpu.roll(y, HALF, 1))
        y = y * cos + partner * sin
        nat_ref[:, cols] = y.astype(bf16)
        if c in (NAT_KA, NAT_KA + 1):
            for b in range(ROW_TILE // MOBA_BLOCK):
                blk = y[b * MOBA_BLOCK:(b + 1) * MOBA_BLOCK]
                km_ref[b, :, (c - NAT_KA) * LANES:(c - NAT_KA + 1) * LANES] = jnp.mean(blk, axis=0, keepdims=True)
    nat_ref[:, NAT_QK_COLS:] = proj[:, NAT_QK_COLS:].astype(bf16)

    proj_t = lax.dot_general(wt_ref[...], h, _NT, preferred_element_type=f32)
    cos_t = cost_ref[...]
    sin_t = sint_ref[...]
    for c in range(QT_ROWS // HEAD_DIM):
        rows = slice(c * HEAD_DIM, (c + 1) * HEAD_DIM)
        y = proj_t[rows, :]
        inv = lax.rsqrt(jnp.mean(y * y, axis=0, keepdims=True) + EPS)
        y = y * inv * hgt_ref[rows, :]
        y1, y2 = y[:HALF], y[HALF:]
        qt_ref[c * HEAD_DIM:c * HEAD_DIM + HALF, :] = (y1 * cos_t - y2 * sin_t).astype(bf16)
        qt_ref[c * HEAD_DIM + HALF:(c + 1) * HEAD_DIM, :] = (y2 * cos_t + y1 * sin_t).astype(bf16)
    vt_ref[...] = proj_t[QT_ROWS:, :].astype(bf16)


def _proj(x, gain, w_nat, w_t, hg_nat, hg_t, cos, sin, cos_t, sin_t):
    s = x.shape[0]
    nb = ROW_TILE // MOBA_BLOCK
    return pl.pallas_call(
        _proj_kernel,
        grid=(s // ROW_TILE,),
        in_specs=[pl.BlockSpec((ROW_TILE, D_MODEL), lambda i: (i, 0)),
                  _const_spec((1, D_MODEL)),
                  _const_spec((D_MODEL, NAT_COLS)),
                  _const_spec((QT_ROWS + VT_ROWS, D_MODEL)),
                  _const_spec((1, NAT_QK_COLS)),
                  _const_spec((QT_ROWS, 1)),
                  pl.BlockSpec((ROW_TILE, LANES), lambda i: (i, 0)),
                  pl.BlockSpec((ROW_TILE, LANES), lambda i: (i, 0)),
                  pl.BlockSpec((HALF, ROW_TILE), lambda i: (0, i)),
                  pl.BlockSpec((HALF, ROW_TILE), lambda i: (0, i))],
        out_specs=[pl.BlockSpec((ROW_TILE, NAT_COLS), lambda i: (i, 0)),
                   pl.BlockSpec((QT_ROWS, ROW_TILE), lambda i: (0, i)),
                   pl.BlockSpec((VT_ROWS, ROW_TILE), lambda i: (0, i)),
                   pl.BlockSpec((nb, 1, 2 * LANES), lambda i: (i, 0, 0))],
        out_shape=[jax.ShapeDtypeStruct((s, NAT_COLS), bf16),
                   jax.ShapeDtypeStruct((QT_ROWS, s), bf16),
                   jax.ShapeDtypeStruct((VT_ROWS, s), bf16),
                   jax.ShapeDtypeStruct((s // MOBA_BLOCK, 1, 2 * LANES), f32)],
        compiler_params=_params(1),
        name="qkv_proj",
    )(x, gain, w_nat, w_t, hg_nat, hg_t, cos, sin, cos_t, sin_t)


def _split_rows(q_t):
    zero = jnp.zeros((HEAD_DIM, q_t.shape[1]), q_t.dtype)
    return (jnp.concatenate([q_t[:HEAD_DIM], zero], axis=0),
            jnp.concatenate([zero, q_t[HEAD_DIM:]], axis=0))


def _flash_init(s_t, v_t, m_ref, l_ref, acc_ref, h):
    m = jnp.max(s_t, axis=0, keepdims=True)
    p = jnp.exp2(s_t - m)
    m_ref[h] = m
    l_ref[h] = jnp.sum(p, axis=0, keepdims=True)
    acc_ref[h] = jnp.dot(v_t, p.astype(bf16), preferred_element_type=f32)


def _flash_step(s_t, v_t, m_ref, l_ref, acc_ref, h):
    m_old = m_ref[h]
    m_new = jnp.maximum(m_old, jnp.max(s_t, axis=0, keepdims=True))
    alpha = jnp.exp2(m_old - m_new)
    p = jnp.exp2(s_t - m_new)
    l_ref[h] = alpha * l_ref[h] + jnp.sum(p, axis=0, keepdims=True)
    acc_ref[h] = alpha * acc_ref[h] + jnp.dot(v_t, p.astype(bf16), preferred_element_type=f32)
    m_ref[h] = m_new


def _positions(first_key, n_keys, t):
    key = first_key + lax.broadcasted_iota(jnp.int32, (n_keys, ATT_TILE), 0)
    query = t * ATT_TILE + lax.broadcasted_iota(jnp.int32, (n_keys, ATT_TILE), 1)
    return key, query


def _moba_kernel(qt_ref, k_ref, vt_ref, km_ref, o_ref, m_ref, l_ref, acc_ref):
    t = pl.program_id(1)
    q_heads = _split_rows(qt_ref[...])
    n_blocks = km_ref.shape[0]

    km = km_ref[...]
    km_hi = km.astype(bf16)
    rem = km - km_hi.astype(f32)
    km_mid = rem.astype(bf16)
    km_lo = (rem - km_mid.astype(f32)).astype(bf16)
    blk = lax.broadcasted_iota(jnp.int32, (n_blocks, 1), 0).astype(f32)
    t_f = t.astype(f32)
    picks = []
    for qh in q_heads:
        gate = (jnp.dot(km_hi, qh, preferred_element_type=f32)
                + jnp.dot(km_mid, qh, preferred_element_type=f32)
                + jnp.dot(km_lo, qh, preferred_element_type=f32))
        gate = jnp.where(blk < t_f, gate, NEG)
        chosen = []
        for _ in range(MOBA_TOPK):
            best = jnp.max(gate, axis=0, keepdims=True)
            idx = jnp.min(jnp.where(gate == best, blk, 1e9), axis=0, keepdims=True)
            chosen.append(jnp.where(best > 0.5 * NEG, idx, -1.0))
            gate = jnp.where(blk == idx, 2.0 * NEG, gate)
        picks.append(chosen)

    def group_scores(g, h, with_own):
        rows = pl.ds(pl.multiple_of(g * KEY_GROUP, KEY_GROUP), KEY_GROUP)
        s_t = jnp.dot(k_ref[rows, :], q_heads[h], preferred_element_type=f32)
        parts = []
        for b in range(GROUP_BLOCKS):
            blk_f = (g * GROUP_BLOCKS + b).astype(f32)
            hit = (picks[h][0] == blk_f) | (picks[h][1] == blk_f) | (picks[h][2] == blk_f)
            s_b = s_t[b * MOBA_BLOCK:(b + 1) * MOBA_BLOCK]
            masked = jnp.where(hit, s_b, NEG)
            if with_own:
                key, query = _positions(g * KEY_GROUP + b * MOBA_BLOCK, MOBA_BLOCK, t)
                masked = jnp.where(key >= t * ATT_TILE, jnp.where(key <= query, s_b, NEG), masked)
            parts.append(masked)
        return jnp.concatenate(parts, axis=0), vt_ref[h * HEAD_DIM:(h + 1) * HEAD_DIM, rows]

    own_group = t // GROUP_BLOCKS
    for h in range(2):
        s_t, v_t = group_scores(own_group, h, True)
        _flash_init(s_t, v_t, m_ref, l_ref, acc_ref, h)

    def past_group(g, carry):
        for h in range(2):
            s_t, v_t = group_scores(g, h, False)
            _flash_step(s_t, v_t, m_ref, l_ref, acc_ref, h)
        return carry

    lax.fori_loop(0, own_group, past_group, 0)

    out_t = jnp.concatenate([acc_ref[0] / l_ref[0], acc_ref[1] / l_ref[1]], axis=0)
    o_ref[...] = out_t.T.astype(bf16)


def _moba(nat, q_t, v_t, kmean):
    s = nat.shape[0]
    n_blocks = s // MOBA_BLOCK
    return pl.pallas_call(
        _moba_kernel,
        grid=(2, s // ATT_TILE),
        in_specs=[pl.BlockSpec((LANES, ATT_TILE), lambda p, t: (p, t)),
                  pl.BlockSpec((s, LANES), lambda p, t: (0, NAT_KA + p)),
                  pl.BlockSpec((LANES, s), lambda p, t: (p, 0)),
                  pl.BlockSpec((n_blocks, LANES), lambda p, t: (0, p))],
        out_specs=pl.BlockSpec((ATT_TILE, LANES), lambda p, t: (t, p)),
        out_shape=jax.ShapeDtypeStruct((s, 2 * LANES), bf16),
        scratch_shapes=[pltpu.VMEM((2, 1, ATT_TILE), f32), pltpu.VMEM((2, 1, ATT_TILE), f32),
                        pltpu.VMEM((2, HEAD_DIM, ATT_TILE), f32)],
        compiler_params=_params(2),
        name="moba_attention",
    )(q_t, nat, v_t, kmean)


def _swa_kernel(sink_ref, q_ref, k_ref, kp_ref, v_ref, vp_ref, o_ref):
    p_id = pl.program_id(0)
    t = pl.program_id(1)
    sub_tiles = SWA_TILE // SWA_WINDOW
    lane = lax.broadcasted_iota(jnp.int32, (1, LANES), 1)
    zero = jnp.zeros((SWA_WINDOW, LANES), bf16)
    qi = lax.broadcasted_iota(jnp.int32, (SWA_WINDOW, 2 * SWA_WINDOW), 0)
    kj = lax.broadcasted_iota(jnp.int32, (SWA_WINDOW, 2 * SWA_WINDOW), 1)
    back = qi + SWA_WINDOW - kj
    in_window = (back >= 0) & (back < SWA_WINDOW)
    for sub in range(sub_tiles):
        rows = slice(sub * SWA_WINDOW, (sub + 1) * SWA_WINDOW)
        if sub == 0:
            k_prev, v_prev = kp_ref[...], vp_ref[...]
            mask = in_window & (kj >= SWA_WINDOW - t * SWA_TILE)
        else:
            prev = slice((sub - 1) * SWA_WINDOW, sub * SWA_WINDOW)
            k_prev, v_prev = k_ref[prev, :], v_ref[prev, :]
            mask = in_window
        k_band = jnp.concatenate([k_prev, k_ref[rows, :]], axis=0)
        v_band = jnp.concatenate([v_prev, v_ref[rows, :]], axis=0)
        q = q_ref[rows, :]
        outs = []
        for h, qh in enumerate((jnp.where(lane < HEAD_DIM, q, zero), jnp.where(lane < HEAD_DIM, zero, q))):
            sink = sink_ref[2 * p_id + h]
            s = lax.dot_general(qh, k_band, _NT, preferred_element_type=f32)
            s = jnp.where(mask, s, NEG)
            m = jnp.maximum(jnp.max(s, axis=-1, keepdims=True), sink)
            p = jnp.exp(s - m)
            denom = jnp.sum(p, axis=-1, keepdims=True) + jnp.exp(sink - m)
            outs.append(jnp.dot(p.astype(bf16), v_band, preferred_element_type=f32) / denom)
        o_ref[rows, :] = jnp.where(lane < HEAD_DIM, outs[0], outs[1]).astype(bf16)


def _swa(nat, sinks):
    s = nat.shape[0]
    ratio = SWA_TILE // SWA_WINDOW

    def prev_rows(t):
        return jnp.maximum(t * ratio - 1, 0)

    return pl.pallas_call(
        _swa_kernel,
        grid=(2, s // SWA_TILE),
        in_specs=[pl.BlockSpec(memory_space=pltpu.SMEM),
                  pl.BlockSpec((SWA_TILE, LANES), lambda p, t: (t, NAT_QB + p)),
                  pl.BlockSpec((SWA_TILE, LANES), lambda p, t: (t, NAT_KB + p)),
                  pl.BlockSpec((SWA_WINDOW, LANES), lambda p, t: (prev_rows(t), NAT_KB + p)),
                  pl.BlockSpec((SWA_TILE, LANES), lambda p, t: (t, NAT_VB + p)),
                  pl.BlockSpec((SWA_WINDOW, LANES), lambda p, t: (prev_rows(t), NAT_VB + p))],
        out_specs=pl.BlockSpec((SWA_TILE, LANES), lambda p, t: (t, p)),
        out_shape=jax.ShapeDtypeStruct((s, 2 * LANES), bf16),
        compiler_params=_params(2),
        name="swa_attention",
    )(sinks, nat, nat, nat, nat, nat)


def _diff_kernel(lam_init, qt_ref, k_ref, vt_ref, lq1_ref, lk1_ref, lq2_ref, lk2_ref, sub_ref,
                 o_ref, m_ref, l_ref, acc_ref):
    t = pl.program_id(1)
    q_maps = _split_rows(qt_ref[...])

    own_group = t // GROUP_BLOCKS
    rows = pl.ds(pl.multiple_of(own_group * KEY_GROUP, KEY_GROUP), KEY_GROUP)
    k_g = k_ref[rows, :]
    v_g = vt_ref[:, rows]
    key, query = _positions(own_group * KEY_GROUP, KEY_GROUP, t)
    causal = key <= query
    for h, qh in enumerate(q_maps):
        s_t = jnp.dot(k_g, qh, preferred_element_type=f32)
        _flash_init(jnp.where(causal, s_t, NEG), v_g, m_ref, l_ref, acc_ref, h)

    def past_group(g, carry):
        rows = pl.ds(pl.multiple_of(g * KEY_GROUP, KEY_GROUP), KEY_GROUP)
        k_g = k_ref[rows, :]
        v_g = vt_ref[:, rows]
        for h, qh in enumerate(q_maps):
            s_t = jnp.dot(k_g, qh, preferred_element_type=f32)
            _flash_step(s_t, v_g, m_ref, l_ref, acc_ref, h)
        return carry

    lax.fori_loop(0, own_group, past_group, 0)

    lam = (jnp.exp(jnp.sum(lq1_ref[...] * lk1_ref[...], axis=-1, keepdims=True))
           - jnp.exp(jnp.sum(lq2_ref[...] * lk2_ref[...], axis=-1, keepdims=True)) + lam_init)
    out_t = acc_ref[0] / l_ref[0] - lam * (acc_ref[1] / l_ref[1])
    out = _rms(out_t.T, sub_ref[...]) * (1.0 - lam_init)
    o_ref[...] = out.astype(bf16)


def _diff(nat, q_t, v_t, lq1, lk1, lq2, lk2, subln, lam_init):
    s = nat.shape[0]
    vec = _const_spec((1, HEAD_DIM))
    return pl.pallas_call(
        functools.partial(_diff_kernel, lam_init),
        grid=(4, s // ATT_TILE),
        in_specs=[pl.BlockSpec((LANES, ATT_TILE), lambda h, t: (2 + h, t)),
                  pl.BlockSpec((s, LANES), lambda h, t: (0, NAT_KC + h)),
                  pl.BlockSpec((LANES, s), lambda h, t: (2 + h, 0)),
                  vec, vec, vec, vec, _const_spec((1, LANES))],
        out_specs=pl.BlockSpec((ATT_TILE, LANES), lambda h, t: (t, h)),
        out_shape=jax.ShapeDtypeStruct((s, 4 * LANES), bf16),
        scratch_shapes=[pltpu.VMEM((2, 1, ATT_TILE), f32), pltpu.VMEM((2, 1, ATT_TILE), f32),
                        pltpu.VMEM((2, LANES, ATT_TILE), f32)],
        compiler_params=_params(2),
        name="diff_attention",
    )(q_t, nat, v_t, lq1, lk1, lq2, lk2, subln)


def _merge_kernel(x_ref, g_ref, wgate_ref, oa_ref, ob_ref, oc_ref, pa_ref, pb_ref, pc_ref, wo_ref, o_ref):
    x = x_ref[...]
    h = _rms(x, g_ref[...]).astype(bf16)
    branches = ((oa_ref, pa_ref), (ob_ref, pb_ref), (oc_ref, pc_ref))
    merged = jnp.zeros_like(x)
    for i, (b_ref, p_ref) in enumerate(branches):
        logits = jnp.dot(h, wgate_ref[:, i * D_MODEL:(i + 1) * D_MODEL], preferred_element_type=f32)
        y = jnp.dot(b_ref[...], p_ref[...], preferred_element_type=f32)
        merged = merged + jax.nn.sigmoid(logits) * y
    o_ref[...] = x + jnp.dot(merged.astype(bf16), wo_ref[...], preferred_element_type=f32)


def _merge(x, gain, w_gate, oa, ob, oc, pa, pb, pc, w_out):
    s = x.shape[0]

    def rows(width):
        return pl.BlockSpec((ROW_TILE, width), lambda i: (i, 0))

    return pl.pallas_call(
        _merge_kernel,
        grid=(s // ROW_TILE,),
        in_specs=[rows(D_MODEL), _const_spec((1, D_MODEL)), _const_spec((D_MODEL, GATE_COLS)),
                  rows(oa.shape[1]), rows(ob.shape[1]), rows(oc.shape[1]),
                  _const_spec(pa.shape), _const_spec(pb.shape), _const_spec(pc.shape),
                  _const_spec((D_MODEL, D_MODEL))],
        out_specs=rows(D_MODEL),
        out_shape=jax.ShapeDtypeStruct(x.shape, f32),
        compiler_params=_params(1),
        name="gated_merge",
    )(x, gain, w_gate, oa, ob, oc, pa, pb, pc, w_out)


def _rope_tables(seq):
    pos = jnp.arange(seq, dtype=f32)
    inv_freq = ROPE_THETA ** (-jnp.arange(0, HEAD_DIM, 2, dtype=f32) / HEAD_DIM)
    ang = pos[:, None] * inv_freq[None, :]
    cos, sin = jnp.cos(ang), jnp.sin(ang)
    return (jnp.tile(cos, (1, 4)), jnp.concatenate([-sin, sin, -sin, sin], axis=-1), cos.T, sin.T)


def _layer_weights(w_in):
    qa, ka, va = w_in[:, 0:256], w_in[:, 256:512], w_in[:, 512:768]
    qb, kb, vb = w_in[:, 768:1024], w_in[:, 1024:1152], w_in[:, 1152:1280]
    qc, kc, vc = w_in[:, 1280:1792], w_in[:, 1792:2304], w_in[:, 2304:2816]
    gates = w_in[:, 2816:]

    def dup(w):
        h0, h1 = w[:, :HEAD_DIM], w[:, HEAD_DIM:]
        return jnp.concatenate([h0, h0, h1, h1], axis=1)

    w_nat = jnp.concatenate([ka, kc, qb, dup(kb), dup(vb)], axis=1)
    w_t = jnp.concatenate([qa, qc, va, vc], axis=1).T
    return w_nat.astype(bf16), w_t.astype(bf16), gates.astype(bf16)


def _head_gains(qa, ka, qb, kb, qc, kc):
    scale = HEAD_DIM ** -0.5
    scale2 = scale * math.log2(math.e)
    nat = jnp.concatenate([jnp.tile(ka, 4), jnp.tile(kc, 8), jnp.tile(qb * scale, 4), jnp.tile(kb, 4)])
    feat = jnp.concatenate([jnp.tile(qa * scale2, 4), jnp.tile(qc * scale2, 8)])
    return nat[None, :], feat[:, None]


def kernel(x, ffn1_norm, ffn1_w_gate, ffn1_w_up, ffn1_w_down, mix_norm, w_in, moba_q_norm, moba_k_norm, swa_q_norm, swa_k_norm, swa_sinks, diff_q_norm, diff_k_norm, diff_lambda_q1, diff_lambda_k1, diff_lambda_q2, diff_lambda_k2, diff_subln, w_branch_a, w_branch_b, w_branch_c, w_out, ffn2_norm, ffn2_w_gate, ffn2_w_up, ffn2_w_down):
    batch, seq, _ = x.shape
    assert batch == 1 and seq % SWA_TILE == 0 and seq % ROW_TILE == 0 and seq % KEY_GROUP == 0
    depth = w_in.shape[0]
    cos, sin, cos_t, sin_t = _rope_tables(seq)
    xs = x[0]
    for l in range(depth):
        lam_init = 0.8 - 0.6 * math.exp(-0.3 * l)
        xs = _ffn(xs, ffn1_norm[l][None], ffn1_w_gate[l].astype(bf16), ffn1_w_up[l].astype(bf16),
                  ffn1_w_down[l].astype(bf16))
        w_nat, w_t, w_gate = _layer_weights(w_in[l])
        hg_nat, hg_t = _head_gains(moba_q_norm[l], moba_k_norm[l], swa_q_norm[l], swa_k_norm[l],
                                   diff_q_norm[l], diff_k_norm[l])
        nat, q_t, v_t, kmean = _proj(xs, mix_norm[l][None], w_nat, w_t, hg_nat, hg_t, cos, sin, cos_t, sin_t)
        oa = _moba(nat, q_t, v_t, kmean.reshape(seq // MOBA_BLOCK, 2 * LANES))
        ob = _swa(nat, swa_sinks[l])
        oc = _diff(nat, q_t, v_t, diff_lambda_q1[l][None], diff_lambda_k1[l][None], diff_lambda_q2[l][None],
                   diff_lambda_k2[l][None], diff_subln[l][None], lam_init)
        xs = _merge(xs, mix_norm[l][None], w_gate, oa, ob, oc, w_branch_a[l].astype(bf16),
                    w_branch_b[l].astype(bf16), w_branch_c[l].astype(bf16), w_out[l].astype(bf16))
        xs = _ffn(xs, ffn2_norm[l][None], ffn2_w_gate[l].astype(bf16), ffn2_w_up[l].astype(bf16),
                  ffn2_w_down[l].astype(bf16))
    return xs[None]
```

```python
import functools
import math

import jax
import jax.numpy as jnp
from jax import lax
from jax.experimental import pallas as pl
from jax.experimental.pallas import tpu as pltpu

D_MODEL = 1024
D_FF = 2816
HEAD_DIM = 64
HALF = HEAD_DIM // 2
LANES = 128
MOBA_BLOCK = 256
MOBA_TOPK = 3
SWA_WINDOW = 128
N_BRANCH = 3
ROPE_THETA = 10000.0
EPS = 1e-6
NEG = -1e30
FFN_HALF = 0.5

NAT_QK_COLS = 1280
NAT_COLS = 1536
NAT_KA, NAT_KC, NAT_QB, NAT_KB, NAT_VB = 0, 2, 6, 8, 10
QT_ROWS = 768
VT_ROWS = 768
GATE_COLS = N_BRANCH * D_MODEL

ROW_TILE = 512
ATT_TILE = 256
GROUP_BLOCKS = 4
KEY_GROUP = GROUP_BLOCKS * MOBA_BLOCK
SWA_TILE = 512
VMEM_LIMIT = 48 * 1024 * 1024

f32 = jnp.float32
bf16 = jnp.bfloat16

_NT = (((1,), (1,)), ((), ()))


def _rms(x, gain):
    return x * lax.rsqrt(jnp.mean(x * x, axis=-1, keepdims=True) + EPS) * gain


def _params(n_axes):
    return pltpu.CompilerParams(
        dimension_semantics=("arbitrary",) * n_axes, vmem_limit_bytes=VMEM_LIMIT)


def _const_spec(shape):
    return pl.BlockSpec(shape, lambda *_: (0,) * len(shape), pipeline_mode=pl.Buffered(1))


def _ffn_kernel(x_ref, g_ref, wg_ref, wu_ref, wd_ref, o_ref):
    x = x_ref[...]
    h = _rms(x, g_ref[...]).astype(bf16)
    a = jnp.dot(h, wg_ref[...], preferred_element_type=f32)
    b = jnp.dot(h, wu_ref[...], preferred_element_type=f32)
    act = (a * jax.nn.sigmoid(a) * b).astype(bf16)
    o_ref[...] = x + FFN_HALF * jnp.dot(act, wd_ref[...], preferred_element_type=f32)


def _ffn(x, gain, wg, wu, wd):
    s = x.shape[0]
    row = pl.BlockSpec((ROW_TILE, D_MODEL), lambda i: (i, 0))
    return pl.pallas_call(
        _ffn_kernel,
        grid=(s // ROW_TILE,),
        in_specs=[row, _const_spec((1, D_MODEL)), _const_spec((D_MODEL, D_FF)),
                  _const_spec((D_MODEL, D_FF)), _const_spec((D_FF, D_MODEL))],
        out_specs=row,
        out_shape=jax.ShapeDtypeStruct(x.shape, f32),
        compiler_params=_params(1),
        name="ffn",
    )(x, gain, wg, wu, wd)


def _proj_kernel(x_ref, g_ref, wn_ref, wt_ref, hg_ref, hgt_ref, cos_ref, sin_ref, cost_ref, sint_ref,
                 nat_ref, qt_ref, vt_ref, km_ref):
    x = x_ref[...]
    h = _rms(x, g_ref[...]).astype(bf16)

    proj = jnp.dot(h, wn_ref[...], preferred_element_type=f32)
    lane = lax.broadcasted_iota(jnp.int32, (1, LANES), 1)
    first_head = lane < HEAD_DIM
    first_half = (lane % HEAD_DIM) < HALF
    cos = cos_ref[...]
    sin = sin_ref[...]
    for c in range(NAT_QK_COLS // LANES):
        cols = slice(c * LANES, (c + 1) * LANES)
        y = proj[:, cols]
        sq = y * y
        ss0 = jnp.sum(jnp.where(first_head, sq, 0.0), axis=-1, keepdims=True)
        ss1 = jnp.sum(jnp.where(first_head, 0.0, sq), axis=-1, keepdims=True)
        inv = jnp.where(first_head, lax.rsqrt(ss0 / HEAD_DIM + EPS), lax.rsqrt(ss1 / HEAD_DIM + EPS))
        y = y * inv * hg_ref[:, cols]
        partner = jnp.where(first_half, pltpu.roll(y, LANES - HALF, 1), pltpu.roll(y, HALF, 1))
        y = y * cos + partner * sin
        nat_ref[:, cols] = y.astype(bf16)
        if c in (NAT_KA, NAT_KA + 1):
            for b in range(ROW_TILE // MOBA_BLOCK):
                blk = y[b * MOBA_BLOCK:(b + 1) * MOBA_BLOCK]
                km_ref[b, :, (c - NAT_KA) * LANES:(c - NAT_KA + 1) * LANES] = jnp.mean(blk, axis=0, keepdims=True)
    nat_ref[:, NAT_QK_COLS:] = proj[:, NAT_QK_COLS:].astype(bf16)

    proj_t = lax.dot_general(wt_ref[...], h, _NT, preferred_element_type=f32)
    cos_t = cost_ref[...]
    sin_t = sint_ref[...]
    for c in range(QT_ROWS // HEAD_DIM):
        rows = slice(c * HEAD_DIM, (c + 1) * HEAD_DIM)
        y = proj_t[rows, :]
        inv = lax.rsqrt(jnp.mean(y * y, axis=0, keepdims=True) + EPS)
        y = y * inv * hgt_ref[rows, :]
        y1, y2 = y[:HALF], y[HALF:]
        qt_ref[c * HEAD_DIM:c * HEAD_DIM + HALF, :] = (y1 * cos_t - y2 * sin_t).astype(bf16)
        qt_ref[c * HEAD_DIM + HALF:(c + 1) * HEAD_DIM, :] = (y2 * cos_t + y1 * sin_t).astype(bf16)
    vt_ref[...] = proj_t[QT_ROWS:, :].astype(bf16)


def _proj(x, gain, w_nat, w_t, hg_nat, hg_t, cos, sin, cos_t, sin_t):
    s = x.shape[0]
    nb = ROW_TILE // MOBA_BLOCK
    return pl.pallas_call(
        _proj_kernel,
        grid=(s // ROW_TILE,),
        in_specs=[pl.BlockSpec((ROW_TILE, D_MODEL), lambda i: (i, 0)),
                  _const_spec((1, D_MODEL)),
                  _const_spec((D_MODEL, NAT_COLS)),
                  _const_spec((QT_ROWS + VT_ROWS, D_MODEL)),
                  _const_spec((1, NAT_QK_COLS)),
                  _const_spec((QT_ROWS, 1)),
                  pl.BlockSpec((ROW_TILE, LANES), lambda i: (i, 0)),
                  pl.BlockSpec((ROW_TILE, LANES), lambda i: (i, 0)),
                  pl.BlockSpec((HALF, ROW_TILE), lambda i: (0, i)),
                  pl.BlockSpec((HALF, ROW_TILE), lambda i: (0, i))],
        out_specs=[pl.BlockSpec((ROW_TILE, NAT_COLS), lambda i: (i, 0)),
                   pl.BlockSpec((QT_ROWS, ROW_TILE), lambda i: (0, i)),
                   pl.BlockSpec((VT_ROWS, ROW_TILE), lambda i: (0, i)),
                   pl.BlockSpec((nb, 1, 2 * LANES), lambda i: (i, 0, 0))],
        out_shape=[jax.ShapeDtypeStruct((s, NAT_COLS), bf16),
                   jax.ShapeDtypeStruct((QT_ROWS, s), bf16),
                   jax.ShapeDtypeStruct((VT_ROWS, s), bf16),
                   jax.ShapeDtypeStruct((s // MOBA_BLOCK, 1, 2 * LANES), f32)],
        compiler_params=_params(1),
        name="qkv_proj",
    )(x, gain, w_nat, w_t, hg_nat, hg_t, cos, sin, cos_t, sin_t)


def _split_rows(q_t):
    zero = jnp.zeros((HEAD_DIM, q_t.shape[1]), q_t.dtype)
    return (jnp.concatenate([q_t[:HEAD_DIM], zero], axis=0),
            jnp.concatenate([zero, q_t[HEAD_DIM:]], axis=0))


class _Slot:
    def __init__(self, s_ref, mx_ref):
        self.s, self.mx = s_ref, mx_ref


def _flash_stage(n_maps, state, score=None, nxt=None, cur=None, value=None):
    m_ref, l_ref, acc_ref = state
    maps = range(n_maps)
    if cur is not None:
        m_old = [m_ref[h] for h in maps]
        m_new = [jnp.maximum(m_old[h], cur.mx[h]) for h in maps]
        alpha = [jnp.exp2(m_old[h] - m_new[h]) for h in maps]
        l_sum = [None] * n_maps
        pv = [None] * n_maps
    col_max = [None] * n_maps
    for b in range(GROUP_BLOCKS):
        rows = slice(b * MOBA_BLOCK, (b + 1) * MOBA_BLOCK)
        if score is not None:
            for h in maps:
                s_b = score(b, h)
                nxt.s[h, rows, :] = s_b
                top = jnp.max(s_b, axis=0, keepdims=True)
                col_max[h] = top if b == 0 else jnp.maximum(col_max[h], top)
        if cur is not None:
            for h in maps:
                p = jnp.exp2(cur.s[h, rows, :] - m_new[h])
                part = jnp.sum(p, axis=0, keepdims=True)
                prod = jnp.dot(value(b, h), p.astype(bf16), preferred_element_type=f32)
                l_sum[h] = part if b == 0 else l_sum[h] + part
                pv[h] = prod if b == 0 else pv[h] + prod
    if score is not None:
        for h in maps:
            nxt.mx[h] = col_max[h]
    if cur is not None:
        for h in maps:
            l_ref[h] = alpha[h] * l_ref[h] + l_sum[h]
            acc_ref[h] = alpha[h] * acc_ref[h] + pv[h]
            m_ref[h] = m_new[h]


def _flash_pipeline(n_maps, own_group, own_score, past_score, value, slots, state):
    slot_a, slot_b = slots
    m_ref, l_ref, acc_ref = state
    m_ref[...] = jnp.full(m_ref.shape, NEG, f32)
    l_ref[...] = jnp.zeros(l_ref.shape, f32)
    acc_ref[...] = jnp.zeros(acc_ref.shape, f32)

    def stage(park, nxt, cur, consumed):
        group = jnp.where(consumed == 0, own_group, consumed - 1)
        score = None if park is None else (lambda b, h: past_score(park, b, h))
        _flash_stage(n_maps, state, score=score, nxt=nxt, cur=cur, value=lambda b, h: value(group, b, h))

    _flash_stage(n_maps, state, score=own_score, nxt=slot_a)

    def pair(k, carry):
        stage(2 * k, slot_b, slot_a, 2 * k)
        stage(2 * k + 1, slot_a, slot_b, 2 * k + 1)
        return carry

    lax.fori_loop(0, own_group // 2, pair, 0)

    @pl.when(own_group % 2 == 1)
    def _():
        stage(own_group - 1, slot_b, slot_a, own_group - 1)
        stage(None, None, slot_b, own_group)

    @pl.when(own_group % 2 == 0)
    def _():
        stage(None, None, slot_a, own_group)


def _flash_scratch(n_maps, dv):
    row = pltpu.VMEM((n_maps, 1, ATT_TILE), f32)
    scores = pltpu.VMEM((n_maps, KEY_GROUP, ATT_TILE), f32)
    return [row, row, pltpu.VMEM((n_maps, dv, ATT_TILE), f32), scores, row, scores, row]


def _positions(first_key, n_keys, t):
    key = first_key + lax.broadcasted_iota(jnp.int32, (n_keys, ATT_TILE), 0)
    query = t * ATT_TILE + lax.broadcasted_iota(jnp.int32, (n_keys, ATT_TILE), 1)
    return key, query


def _moba_kernel(qt_ref, k_ref, vt_ref, km_ref, o_ref, m_ref, l_ref, acc_ref, sa_ref, mxa_ref, sb_ref, mxb_ref):
    t = pl.program_id(1)
    q_heads = _split_rows(qt_ref[...])
    n_blocks = km_ref.shape[0]

    km = km_ref[...]
    km_hi = km.astype(bf16)
    rem = km - km_hi.astype(f32)
    km_mid = rem.astype(bf16)
    km_lo = (rem - km_mid.astype(f32)).astype(bf16)
    blk = lax.broadcasted_iota(jnp.int32, (n_blocks, 1), 0).astype(f32)
    t_f = t.astype(f32)
    picks = []
    for qh in q_heads:
        gate = (jnp.dot(km_hi, qh, preferred_element_type=f32)
                + jnp.dot(km_mid, qh, preferred_element_type=f32)
                + jnp.dot(km_lo, qh, preferred_element_type=f32))
        gate = jnp.where(blk < t_f, gate, NEG)
        chosen = []
        for _ in range(MOBA_TOPK):
            best = jnp.max(gate, axis=0, keepdims=True)
            idx = jnp.min(jnp.where(gate == best, blk, 1e9), axis=0, keepdims=True)
            chosen.append(jnp.where(best > 0.5 * NEG, idx, -1.0))
            gate = jnp.where(blk == idx, 2.0 * NEG, gate)
        picks.append(chosen)

    own_group = t // GROUP_BLOCKS

    def block_rows(g, b):
        return pl.ds(pl.multiple_of((g * GROUP_BLOCKS + b) * MOBA_BLOCK, MOBA_BLOCK), MOBA_BLOCK)

    def past_score(g, b, h):
        s_b = jnp.dot(k_ref[block_rows(g, b), :], q_heads[h], preferred_element_type=f32)
        blk_f = (g * GROUP_BLOCKS + b).astype(f32)
        hit = (picks[h][0] == blk_f) | (picks[h][1] == blk_f) | (picks[h][2] == blk_f)
        return s_b, jnp.where(hit, s_b, NEG)

    def own_score(b, h):
        s_b, masked = past_score(own_group, b, h)
        key, query = _positions((own_group * GROUP_BLOCKS + b) * MOBA_BLOCK, MOBA_BLOCK, t)
        return jnp.where(key >= t * ATT_TILE, jnp.where(key <= query, s_b, NEG), masked)

    def value(g, b, h):
        return vt_ref[h * HEAD_DIM:(h + 1) * HEAD_DIM, block_rows(g, b)]

    _flash_pipeline(2, own_group, own_score, lambda g, b, h: past_score(g, b, h)[1], value,
                    (_Slot(sa_ref, mxa_ref), _Slot(sb_ref, mxb_ref)), (m_ref, l_ref, acc_ref))

    out_t = jnp.concatenate([acc_ref[0] / l_ref[0], acc_ref[1] / l_ref[1]], axis=0)
    o_ref[...] = out_t.T.astype(bf16)


def _moba(nat, q_t, v_t, kmean):
    s = nat.shape[0]
    n_blocks = s // MOBA_BLOCK
    return pl.pallas_call(
        _moba_kernel,
        grid=(2, s // ATT_TILE),
        in_specs=[pl.BlockSpec((LANES, ATT_TILE), lambda p, t: (p, t)),
                  pl.BlockSpec((s, LANES), lambda p, t: (0, NAT_KA + p)),
                  pl.BlockSpec((LANES, s), lambda p, t: (p, 0)),
                  pl.BlockSpec((n_blocks, LANES), lambda p, t: (0, p))],
        out_specs=pl.BlockSpec((ATT_TILE, LANES), lambda p, t: (t, p)),
        out_shape=jax.ShapeDtypeStruct((s, 2 * LANES), bf16),
        scratch_shapes=_flash_scratch(2, HEAD_DIM),
        compiler_params=_params(2),
        name="moba_attention",
    )(q_t, nat, v_t, kmean)


def _swa_kernel(sink_ref, q_ref, k_ref, kp_ref, v_ref, vp_ref, o_ref):
    p_id = pl.program_id(0)
    t = pl.program_id(1)
    sub_tiles = SWA_TILE // SWA_WINDOW
    lane = lax.broadcasted_iota(jnp.int32, (1, LANES), 1)
    zero = jnp.zeros((SWA_WINDOW, LANES), bf16)
    qi = lax.broadcasted_iota(jnp.int32, (SWA_WINDOW, 2 * SWA_WINDOW), 0)
    kj = lax.broadcasted_iota(jnp.int32, (SWA_WINDOW, 2 * SWA_WINDOW), 1)
    back = qi + SWA_WINDOW - kj
    in_window = (back >= 0) & (back < SWA_WINDOW)
    for sub in range(sub_tiles):
        rows = slice(sub * SWA_WINDOW, (sub + 1) * SWA_WINDOW)
        if sub == 0:
            k_prev, v_prev = kp_ref[...], vp_ref[...]
            mask = in_window & (kj >= SWA_WINDOW - t * SWA_TILE)
        else:
            prev = slice((sub - 1) * SWA_WINDOW, sub * SWA_WINDOW)
            k_prev, v_prev = k_ref[prev, :], v_ref[prev, :]
            mask = in_window
        k_band = jnp.concatenate([k_prev, k_ref[rows, :]], axis=0)
        v_band = jnp.concatenate([v_prev, v_ref[rows, :]], axis=0)
        q = q_ref[rows, :]
        outs = []
        for h, qh in enumerate((jnp.where(lane < HEAD_DIM, q, zero), jnp.where(lane < HEAD_DIM, zero, q))):
            sink = sink_ref[2 * p_id + h]
            s = lax.dot_general(qh, k_band, _NT, preferred_element_type=f32)
            s = jnp.where(mask, s, NEG)
            m = jnp.maximum(jnp.max(s, axis=-1, keepdims=True), sink)
            p = jnp.exp(s - m)
            denom = jnp.sum(p, axis=-1, keepdims=True) + jnp.exp(sink - m)
            outs.append(jnp.dot(p.astype(bf16), v_band, preferred_element_type=f32) / denom)
        o_ref[rows, :] = jnp.where(lane < HEAD_DIM, outs[0], outs[1]).astype(bf16)


def _swa(nat, sinks):
    s = nat.shape[0]
    ratio = SWA_TILE // SWA_WINDOW

    def prev_rows(t):
        return jnp.maximum(t * ratio - 1, 0)

    return pl.pallas_call(
        _swa_kernel,
        grid=(2, s // SWA_TILE),
        in_specs=[pl.BlockSpec(memory_space=pltpu.SMEM),
                  pl.BlockSpec((SWA_TILE, LANES), lambda p, t: (t, NAT_QB + p)),
                  pl.BlockSpec((SWA_TILE, LANES), lambda p, t: (t, NAT_KB + p)),
                  pl.BlockSpec((SWA_WINDOW, LANES), lambda p, t: (prev_rows(t), NAT_KB + p)),
                  pl.BlockSpec((SWA_TILE, LANES), lambda p, t: (t, NAT_VB + p)),
                  pl.BlockSpec((SWA_WINDOW, LANES), lambda p, t: (prev_rows(t), NAT_VB + p))],
        out_specs=pl.BlockSpec((SWA_TILE, LANES), lambda p, t: (t, p)),
        out_shape=jax.ShapeDtypeStruct((s, 2 * LANES), bf16),
        compiler_params=_params(2),
        name="swa_attention",
    )(sinks, nat, nat, nat, nat, nat)


def _diff_kernel(lam_init, qt_ref, k_ref, vt_ref, lq1_ref, lk1_ref, lq2_ref, lk2_ref, sub_ref,
                 o_ref, m_ref, l_ref, acc_ref, sa_ref, mxa_ref, sb_ref, mxb_ref):
    t = pl.program_id(1)
    q_maps = _split_rows(qt_ref[...])

    own_group = t // GROUP_BLOCKS

    def block_rows(g, b):
        return pl.ds(pl.multiple_of((g * GROUP_BLOCKS + b) * MOBA_BLOCK, MOBA_BLOCK), MOBA_BLOCK)

    def past_score(g, b, h):
        return jnp.dot(k_ref[block_rows(g, b), :], q_maps[h], preferred_element_type=f32)

    def own_score(b, h):
        key, query = _positions((own_group * GROUP_BLOCKS + b) * MOBA_BLOCK, MOBA_BLOCK, t)
        return jnp.where(key <= query, past_score(own_group, b, h), NEG)

    def value(g, b, h):
        return vt_ref[:, block_rows(g, b)]

    _flash_pipeline(2, own_group, own_score, past_score, value,
                    (_Slot(sa_ref, mxa_ref), _Slot(sb_ref, mxb_ref)), (m_ref, l_ref, acc_ref))

    lam = (jnp.exp(jnp.sum(lq1_ref[...] * lk1_ref[...], axis=-1, keepdims=True))
           - jnp.exp(jnp.sum(lq2_ref[...] * lk2_ref[...], axis=-1, keepdims=True)) + lam_init)
    out_t = acc_ref[0] / l_ref[0] - lam * (acc_ref[1] / l_ref[1])
    out = _rms(out_t.T, sub_ref[...]) * (1.0 - lam_init)
    o_ref[...] = out.astype(bf16)


def _diff(nat, q_t, v_t, lq1, lk1, lq2, lk2, subln, lam_init):
    s = nat.shape[0]
    vec = _const_spec((1, HEAD_DIM))
    return pl.pallas_call(
        functools.partial(_diff_kernel, lam_init),
        grid=(4, s // ATT_TILE),
        in_specs=[pl.BlockSpec((LANES, ATT_TILE), lambda h, t: (2 + h, t)),
                  pl.BlockSpec((s, LANES), lambda h, t: (0, NAT_KC + h)),
                  pl.BlockSpec((LANES, s), lambda h, t: (2 + h, 0)),
                  vec, vec, vec, vec, _const_spec((1, LANES))],
        out_specs=pl.BlockSpec((ATT_TILE, LANES), lambda h, t: (t, h)),
        out_shape=jax.ShapeDtypeStruct((s, 4 * LANES), bf16),
        scratch_shapes=_flash_scratch(2, LANES),
        compiler_params=_params(2),
        name="diff_attention",
    )(q_t, nat, v_t, lq1, lk1, lq2, lk2, subln)


def _merge_kernel(x_ref, g_ref, wgate_ref, oa_ref, ob_ref, oc_ref, pa_ref, pb_ref, pc_ref, wo_ref, o_ref):
    x = x_ref[...]
    h = _rms(x, g_ref[...]).astype(bf16)
    branches = ((oa_ref, pa_ref), (ob_ref, pb_ref), (oc_ref, pc_ref))
    merged = jnp.zeros_like(x)
    for i, (b_ref, p_ref) in enumerate(branches):
        logits = jnp.dot(h, wgate_ref[:, i * D_MODEL:(i + 1) * D_MODEL], preferred_element_type=f32)
        y = jnp.dot(b_ref[...], p_ref[...], preferred_element_type=f32)
        merged = merged + jax.nn.sigmoid(logits) * y
    o_ref[...] = x + jnp.dot(merged.astype(bf16), wo_ref[...], preferred_element_type=f32)


def _merge(x, gain, w_gate, oa, ob, oc, pa, pb, pc, w_out):
    s = x.shape[0]

    def rows(width):
        return pl.BlockSpec((ROW_TILE, width), lambda i: (i, 0))

    return pl.pallas_call(
        _merge_kernel,
        grid=(s // ROW_TILE,),
        in_specs=[rows(D_MODEL), _const_spec((1, D_MODEL)), _const_spec((D_MODEL, GATE_COLS)),
                  rows(oa.shape[1]), rows(ob.shape[1]), rows(oc.shape[1]),
                  _const_spec(pa.shape), _const_spec(pb.shape), _const_spec(pc.shape),
                  _const_spec((D_MODEL, D_MODEL))],
        out_specs=rows(D_MODEL),
        out_shape=jax.ShapeDtypeStruct(x.shape, f32),
        compiler_params=_params(1),
        name="gated_merge",
    )(x, gain, w_gate, oa, ob, oc, pa, pb, pc, w_out)


def _rope_tables(seq):
    pos = jnp.arange(seq, dtype=f32)
    inv_freq = ROPE_THETA ** (-jnp.arange(0, HEAD_DIM, 2, dtype=f32) / HEAD_DIM)
    ang = pos[:, None] * inv_freq[None, :]
    cos, sin = jnp.cos(ang), jnp.sin(ang)
    return (jnp.tile(cos, (1, 4)), jnp.concatenate([-sin, sin, -sin, sin], axis=-1), cos.T, sin.T)


def _layer_weights(w_in):
    qa, ka, va = w_in[:, 0:256], w_in[:, 256:512], w_in[:, 512:768]
    qb, kb, vb = w_in[:, 768:1024], w_in[:, 1024:1152], w_in[:, 1152:1280]
    qc, kc, vc = w_in[:, 1280:1792], w_in[:, 1792:2304], w_in[:, 2304:2816]
    gates = w_in[:, 2816:]

    def dup(w):
        h0, h1 = w[:, :HEAD_DIM], w[:, HEAD_DIM:]
        return jnp.concatenate([h0, h0, h1, h1], axis=1)

    w_nat = jnp.concatenate([ka, kc, qb, dup(kb), dup(vb)], axis=1)
    w_t = jnp.concatenate([qa, qc, va, vc], axis=1).T
    return w_nat.astype(bf16), w_t.astype(bf16), gates.astype(bf16)


def _head_gains(qa, ka, qb, kb, qc, kc):
    scale = HEAD_DIM ** -0.5
    scale2 = scale * math.log2(math.e)
    nat = jnp.concatenate([jnp.tile(ka, 4), jnp.tile(kc, 8), jnp.tile(qb * scale, 4), jnp.tile(kb, 4)])
    feat = jnp.concatenate([jnp.tile(qa * scale2, 4), jnp.tile(qc * scale2, 8)])
    return nat[None, :], feat[:, None]


def kernel(x, ffn1_norm, ffn1_w_gate, ffn1_w_up, ffn1_w_down, mix_norm, w_in, moba_q_norm, moba_k_norm, swa_q_norm, swa_k_norm, swa_sinks, diff_q_norm, diff_k_norm, diff_lambda_q1, diff_lambda_k1, diff_lambda_q2, diff_lambda_k2, diff_subln, w_branch_a, w_branch_b, w_branch_c, w_out, ffn2_norm, ffn2_w_gate, ffn2_w_up, ffn2_w_down):
    batch, seq, _ = x.shape
    assert batch == 1 and seq % SWA_TILE == 0 and seq % ROW_TILE == 0 and seq % KEY_GROUP == 0
    depth = w_in.shape[0]
    cos, sin, cos_t, sin_t = _rope_tables(seq)
    xs = x[0]
    for l in range(depth):
        lam_init = 0.8 - 0.6 * math.exp(-0.3 * l)
        xs = _ffn(xs, ffn1_norm[l][None], ffn1_w_gate[l].astype(bf16), ffn1_w_up[l].astype(bf16),
                  ffn1_w_down[l].astype(bf16))
        w_nat, w_t, w_gate = _layer_weights(w_in[l])
        hg_nat, hg_t = _head_gains(moba_q_norm[l], moba_k_norm[l], swa_q_norm[l], swa_k_norm[l],
                                   diff_q_norm[l], diff_k_norm[l])
        nat, q_t, v_t, kmean = _proj(xs, mix_norm[l][None], w_nat, w_t, hg_nat, hg_t, cos, sin, cos_t, sin_t)
        oa = _moba(nat, q_t, v_t, kmean.reshape(seq // MOBA_BLOCK, 2 * LANES))
        ob = _swa(nat, swa_sinks[l])
        oc = _diff(nat, q_t, v_t, diff_lambda_q1[l][None], diff_lambda_k1[l][None], diff_lambda_q2[l][None],
                   diff_lambda_k2[l][None], diff_subln[l][None], lam_init)
        xs = _merge(xs, mix_norm[l][None], w_gate, oa, ob, oc, w_branch_a[l].astype(bf16),
                    w_branch_b[l].astype(bf16), w_branch_c[l].astype(bf16), w_out[l].astype(bf16))
        xs = _ffn(xs, ffn2_norm[l][None], ffn2_w_gate[l].astype(bf16), ffn2_w_up[l].astype(bf16),
                  ffn2_w_down[l].astype(bf16))
    return xs[None]
```

```python
import functools
import math

import jax
import jax.numpy as jnp
from jax import lax
from jax.experimental import pallas as pl
from jax.experimental.pallas import tpu as pltpu

D_MODEL = 1024
D_FF = 2816
HEAD_DIM = 64
HALF = HEAD_DIM // 2
LANES = 128
MOBA_BLOCK = 256
MOBA_TOPK = 3
SWA_WINDOW = 128
N_BRANCH = 3
ROPE_THETA = 10000.0
EPS = 1e-6
NEG = -1e30
FFN_HALF = 0.5

NAT_QK_COLS = 1280
NAT_COLS = 1536
NAT_KA, NAT_KC, NAT_QB, NAT_KB, NAT_VB = 0, 2, 6, 8, 10
QT_ROWS = 768
VT_ROWS = 768
GATE_COLS = N_BRANCH * D_MODEL

ROW_TILE = 512
ATT_TILE = 256
GROUP_BLOCKS = 4
KEY_GROUP = GROUP_BLOCKS * MOBA_BLOCK
STAGES_PER_TRIP = 4
PARK_LEAD = 1
SUM_ROWS = 16
SWA_TILE = 512
VMEM_LIMIT = 48 * 1024 * 1024

f32 = jnp.float32
bf16 = jnp.bfloat16

_NT = (((1,), (1,)), ((), ()))


def _rms(x, gain):
    return x * lax.rsqrt(jnp.mean(x * x, axis=-1, keepdims=True) + EPS) * gain


def _params(n_axes):
    return pltpu.CompilerParams(
        dimension_semantics=("arbitrary",) * n_axes, vmem_limit_bytes=VMEM_LIMIT)


def _const_spec(shape):
    return pl.BlockSpec(shape, lambda *_: (0,) * len(shape), pipeline_mode=pl.Buffered(1))


def _ffn_kernel(x_ref, g_ref, wg_ref, wu_ref, wd_ref, o_ref):
    x = x_ref[...]
    h = _rms(x, g_ref[...]).astype(bf16)
    a = jnp.dot(h, wg_ref[...], preferred_element_type=f32)
    b = jnp.dot(h, wu_ref[...], preferred_element_type=f32)
    act = (a * jax.nn.sigmoid(a) * b).astype(bf16)
    o_ref[...] = x + FFN_HALF * jnp.dot(act, wd_ref[...], preferred_element_type=f32)


def _ffn(x, gain, wg, wu, wd):
    s = x.shape[0]
    row = pl.BlockSpec((ROW_TILE, D_MODEL), lambda i: (i, 0))
    return pl.pallas_call(
        _ffn_kernel,
        grid=(s // ROW_TILE,),
        in_specs=[row, _const_spec((1, D_MODEL)), _const_spec((D_MODEL, D_FF)),
                  _const_spec((D_MODEL, D_FF)), _const_spec((D_FF, D_MODEL))],
        out_specs=row,
        out_shape=jax.ShapeDtypeStruct(x.shape, f32),
        compiler_params=_params(1),
        name="ffn",
    )(x, gain, wg, wu, wd)


def _proj_kernel(x_ref, g_ref, wn_ref, wt_ref, hg_ref, hgt_ref, cos_ref, sin_ref, cost_ref, sint_ref,
                 nat_ref, qt_ref, vt_ref, km_ref):
    x = x_ref[...]
    h = _rms(x, g_ref[...]).astype(bf16)

    proj = jnp.dot(h, wn_ref[...], preferred_element_type=f32)
    lane = lax.broadcasted_iota(jnp.int32, (1, LANES), 1)
    first_head = lane < HEAD_DIM
    first_half = (lane % HEAD_DIM) < HALF
    cos = cos_ref[...]
    sin = sin_ref[...]
    for c in range(NAT_QK_COLS // LANES):
        cols = slice(c * LANES, (c + 1) * LANES)
        y = proj[:, cols]
        sq = y * y
        ss0 = jnp.sum(jnp.where(first_head, sq, 0.0), axis=-1, keepdims=True)
        ss1 = jnp.sum(jnp.where(first_head, 0.0, sq), axis=-1, keepdims=True)
        inv = jnp.where(first_head, lax.rsqrt(ss0 / HEAD_DIM + EPS), lax.rsqrt(ss1 / HEAD_DIM + EPS))
        y = y * inv * hg_ref[:, cols]
        partner = jnp.where(first_half, pltpu.roll(y, LANES - HALF, 1), pltpu.roll(y, HALF, 1))
        y = y * cos + partner * sin
        nat_ref[:, cols] = y.astype(bf16)
        if c in (NAT_KA, NAT_KA + 1):
            for b in range(ROW_TILE // MOBA_BLOCK):
                blk = y[b * MOBA_BLOCK:(b + 1) * MOBA_BLOCK]
                km_ref[b, :, (c - NAT_KA) * LANES:(c - NAT_KA + 1) * LANES] = jnp.mean(blk, axis=0, keepdims=True)
    nat_ref[:, NAT_QK_COLS:] = proj[:, NAT_QK_COLS:].astype(bf16)

    proj_t = lax.dot_general(wt_ref[...], h, _NT, preferred_element_type=f32)
    cos_t = cost_ref[...]
    sin_t = sint_ref[...]
    for c in range(QT_ROWS // HEAD_DIM):
        rows = slice(c * HEAD_DIM, (c + 1) * HEAD_DIM)
        y = proj_t[rows, :]
        inv = lax.rsqrt(jnp.mean(y * y, axis=0, keepdims=True) + EPS)
        y = y * inv * hgt_ref[rows, :]
        y1, y2 = y[:HALF], y[HALF:]
        qt_ref[c * HEAD_DIM:c * HEAD_DIM + HALF, :] = (y1 * cos_t - y2 * sin_t).astype(bf16)
        qt_ref[c * HEAD_DIM + HALF:(c + 1) * HEAD_DIM, :] = (y2 * cos_t + y1 * sin_t).astype(bf16)
    vt_ref[...] = proj_t[QT_ROWS:, :].astype(bf16)


def _proj(x, gain, w_nat, w_t, hg_nat, hg_t, cos, sin, cos_t, sin_t):
    s = x.shape[0]
    nb = ROW_TILE // MOBA_BLOCK
    return pl.pallas_call(
        _proj_kernel,
        grid=(s // ROW_TILE,),
        in_specs=[pl.BlockSpec((ROW_TILE, D_MODEL), lambda i: (i, 0)),
                  _const_spec((1, D_MODEL)),
                  _const_spec((D_MODEL, NAT_COLS)),
                  _const_spec((QT_ROWS + VT_ROWS, D_MODEL)),
                  _const_spec((1, NAT_QK_COLS)),
                  _const_spec((QT_ROWS, 1)),
                  pl.BlockSpec((ROW_TILE, LANES), lambda i: (i, 0)),
                  pl.BlockSpec((ROW_TILE, LANES), lambda i: (i, 0)),
                  pl.BlockSpec((HALF, ROW_TILE), lambda i: (0, i)),
                  pl.BlockSpec((HALF, ROW_TILE), lambda i: (0, i))],
        out_specs=[pl.BlockSpec((ROW_TILE, NAT_COLS), lambda i: (i, 0)),
                   pl.BlockSpec((QT_ROWS, ROW_TILE), lambda i: (0, i)),
                   pl.BlockSpec((VT_ROWS, ROW_TILE), lambda i: (0, i)),
                   pl.BlockSpec((nb, 1, 2 * LANES), lambda i: (i, 0, 0))],
        out_shape=[jax.ShapeDtypeStruct((s, NAT_COLS), bf16),
                   jax.ShapeDtypeStruct((QT_ROWS, s), bf16),
                   jax.ShapeDtypeStruct((VT_ROWS, s), bf16),
                   jax.ShapeDtypeStruct((s // MOBA_BLOCK, 1, 2 * LANES), f32)],
        compiler_params=_params(1),
        name="qkv_proj",
    )(x, gain, w_nat, w_t, hg_nat, hg_t, cos, sin, cos_t, sin_t)


def _split_rows(q_t):
    zero = jnp.zeros((HEAD_DIM, q_t.shape[1]), q_t.dtype)
    return (jnp.concatenate([q_t[:HEAD_DIM], zero], axis=0),
            jnp.concatenate([zero, q_t[HEAD_DIM:]], axis=0))


class _Slot:
    def __init__(self, s_ref, mx_ref):
        self.s, self.mx = s_ref, mx_ref


def _with_ones(v_t):
    return jnp.concatenate([v_t, jnp.ones((SUM_ROWS, v_t.shape[1]), v_t.dtype)], axis=0)


def _flash_stage(n_maps, state, score=None, nxt=None, cur=None, value=None):
    m_ref, acc_ref = state
    maps = range(n_maps)
    if cur is not None:
        m_old = [m_ref[h] for h in maps]
        m_new = [jnp.maximum(m_old[h], cur.mx[h]) for h in maps]
        for h in maps:
            acc_ref[h] = jnp.exp2(m_old[h] - m_new[h]) * acc_ref[h]
            m_ref[h] = m_new[h]
    col_max = [None] * n_maps

    def park(b):
        rows = slice(b * MOBA_BLOCK, (b + 1) * MOBA_BLOCK)
        for h in maps:
            s_b = score(b, h)
            nxt.s[h, rows, :] = s_b
            top = jnp.max(s_b, axis=0, keepdims=True)
            col_max[h] = top if b == 0 else jnp.maximum(col_max[h], top)
        if b == GROUP_BLOCKS - 1:
            for h in maps:
                nxt.mx[h] = col_max[h]

    def consume(b):
        rows = slice(b * MOBA_BLOCK, (b + 1) * MOBA_BLOCK)
        for h in maps:
            p = jnp.exp2(cur.s[h, rows, :] - m_new[h]).astype(bf16)
            acc_ref[h] += jnp.dot(_with_ones(value(b, h)), p, preferred_element_type=f32)

    for step in range(GROUP_BLOCKS + PARK_LEAD):
        if score is not None and step < GROUP_BLOCKS:
            park(step)
        if cur is not None and step >= PARK_LEAD:
            consume(step - PARK_LEAD)


def _flash_pipeline(n_maps, own_group, own_score, past_score, value, slots, state):
    m_ref, acc_ref = state
    m_ref[...] = jnp.full(m_ref.shape, NEG, f32)
    acc_ref[...] = jnp.zeros(acc_ref.shape, f32)

    def stage(base, offset, park):
        i = base + offset
        cur, nxt = (slots[0], slots[1]) if offset % 2 == 0 else (slots[1], slots[0])
        group = jnp.where(i == 0, own_group, i - 1)
        score = (lambda b, h: past_score(i, b, h)) if park else None
        _flash_stage(n_maps, state, score=score, nxt=nxt if park else None, cur=cur,
                     value=lambda b, h: value(group, b, h))

    _flash_stage(n_maps, state, score=own_score, nxt=slots[0])

    def trip(k, carry):
        for offset in range(STAGES_PER_TRIP):
            stage(k * STAGES_PER_TRIP, offset, True)
        return carry

    full_trips = own_group // STAGES_PER_TRIP
    lax.fori_loop(0, full_trips, trip, 0)

    base = full_trips * STAGES_PER_TRIP
    left = own_group - base
    for offset in range(STAGES_PER_TRIP):
        if offset > 0:
            pl.when(left >= offset)(functools.partial(stage, base, offset - 1, True))
        pl.when(left == offset)(functools.partial(stage, base, offset, False))


def _flash_scratch(n_maps, dv):
    row = pltpu.VMEM((n_maps, 1, ATT_TILE), f32)
    scores = pltpu.VMEM((n_maps, KEY_GROUP, ATT_TILE), f32)
    return [row, pltpu.VMEM((n_maps, dv + SUM_ROWS, ATT_TILE), f32), scores, row, scores, row]


def _positions(first_key, n_keys, t):
    key = first_key + lax.broadcasted_iota(jnp.int32, (n_keys, ATT_TILE), 0)
    query = t * ATT_TILE + lax.broadcasted_iota(jnp.int32, (n_keys, ATT_TILE), 1)
    return key, query


def _moba_kernel(qt_ref, k_ref, vt_ref, km_ref, o_ref, m_ref, acc_ref, sa_ref, mxa_ref, sb_ref, mxb_ref):
    t = pl.program_id(1)
    q_heads = _split_rows(qt_ref[...])
    n_blocks = km_ref.shape[0]

    km = km_ref[...]
    km_hi = km.astype(bf16)
    rem = km - km_hi.astype(f32)
    km_mid = rem.astype(bf16)
    km_lo = (rem - km_mid.astype(f32)).astype(bf16)
    blk = lax.broadcasted_iota(jnp.int32, (n_blocks, 1), 0).astype(f32)
    t_f = t.astype(f32)
    picks = []
    for qh in q_heads:
        gate = (jnp.dot(km_hi, qh, preferred_element_type=f32)
                + jnp.dot(km_mid, qh, preferred_element_type=f32)
                + jnp.dot(km_lo, qh, preferred_element_type=f32))
        gate = jnp.where(blk < t_f, gate, NEG)
        chosen = []
        for _ in range(MOBA_TOPK):
            best = jnp.max(gate, axis=0, keepdims=True)
            idx = jnp.min(jnp.where(gate == best, blk, 1e9), axis=0, keepdims=True)
            chosen.append(jnp.where(best > 0.5 * NEG, idx, -1.0))
            gate = jnp.where(blk == idx, 2.0 * NEG, gate)
        picks.append(chosen)

    own_group = t // GROUP_BLOCKS

    def block_rows(g, b):
        return pl.ds(pl.multiple_of((g * GROUP_BLOCKS + b) * MOBA_BLOCK, MOBA_BLOCK), MOBA_BLOCK)

    def past_score(g, b, h):
        s_b = jnp.dot(k_ref[block_rows(g, b), :], q_heads[h], preferred_element_type=f32)
        blk_f = (g * GROUP_BLOCKS + b).astype(f32)
        hit = (picks[h][0] == blk_f) | (picks[h][1] == blk_f) | (picks[h][2] == blk_f)
        return s_b, jnp.where(hit, s_b, NEG)

    def own_score(b, h):
        s_b, masked = past_score(own_group, b, h)
        key, query = _positions((own_group * GROUP_BLOCKS + b) * MOBA_BLOCK, MOBA_BLOCK, t)
        return jnp.where(key >= t * ATT_TILE, jnp.where(key <= query, s_b, NEG), masked)

    def value(g, b, h):
        return vt_ref[h * HEAD_DIM:(h + 1) * HEAD_DIM, block_rows(g, b)]

    _flash_pipeline(2, own_group, own_score, lambda g, b, h: past_score(g, b, h)[1], value,
                    (_Slot(sa_ref, mxa_ref), _Slot(sb_ref, mxb_ref)), (m_ref, acc_ref))

    heads = [acc_ref[h, :HEAD_DIM, :] / acc_ref[h, HEAD_DIM:HEAD_DIM + 1, :] for h in range(2)]
    o_ref[...] = jnp.concatenate(heads, axis=0).T.astype(bf16)


def _moba(nat, q_t, v_t, kmean):
    s = nat.shape[0]
    n_blocks = s // MOBA_BLOCK
    return pl.pallas_call(
        _moba_kernel,
        grid=(2, s // ATT_TILE),
        in_specs=[pl.BlockSpec((LANES, ATT_TILE), lambda p, t: (p, t)),
                  pl.BlockSpec((s, LANES), lambda p, t: (0, NAT_KA + p)),
                  pl.BlockSpec((LANES, s), lambda p, t: (p, 0)),
                  pl.BlockSpec((n_blocks, LANES), lambda p, t: (0, p))],
        out_specs=pl.BlockSpec((ATT_TILE, LANES), lambda p, t: (t, p)),
        out_shape=jax.ShapeDtypeStruct((s, 2 * LANES), bf16),
        scratch_shapes=_flash_scratch(2, HEAD_DIM),
        compiler_params=_params(2),
        name="moba_attention",
    )(q_t, nat, v_t, kmean)


def _swa_kernel(sink_ref, q_ref, k_ref, kp_ref, v_ref, vp_ref, o_ref):
    p_id = pl.program_id(0)
    t = pl.program_id(1)
    sub_tiles = SWA_TILE // SWA_WINDOW
    lane = lax.broadcasted_iota(jnp.int32, (1, LANES), 1)
    zero = jnp.zeros((SWA_WINDOW, LANES), bf16)
    qi = lax.broadcasted_iota(jnp.int32, (SWA_WINDOW, 2 * SWA_WINDOW), 0)
    kj = lax.broadcasted_iota(jnp.int32, (SWA_WINDOW, 2 * SWA_WINDOW), 1)
    back = qi + SWA_WINDOW - kj
    in_window = (back >= 0) & (back < SWA_WINDOW)
    for sub in range(sub_tiles):
        rows = slice(sub * SWA_WINDOW, (sub + 1) * SWA_WINDOW)
        if sub == 0:
            k_prev, v_prev = kp_ref[...], vp_ref[...]
            mask = in_window & (kj >= SWA_WINDOW - t * SWA_TILE)
        else:
            prev = slice((sub - 1) * SWA_WINDOW, sub * SWA_WINDOW)
            k_prev, v_prev = k_ref[prev, :], v_ref[prev, :]
            mask = in_window
        k_band = jnp.concatenate([k_prev, k_ref[rows, :]], axis=0)
        v_band = jnp.concatenate([v_prev, v_ref[rows, :]], axis=0)
        q = q_ref[rows, :]
        outs = []
        for h, qh in enumerate((jnp.where(lane < HEAD_DIM, q, zero), jnp.where(lane < HEAD_DIM, zero, q))):
            sink = sink_ref[2 * p_id + h]
            s = lax.dot_general(qh, k_band, _NT, preferred_element_type=f32)
            s = jnp.where(mask, s, NEG)
            m = jnp.maximum(jnp.max(s, axis=-1, keepdims=True), sink)
            p = jnp.exp(s - m)
            denom = jnp.sum(p, axis=-1, keepdims=True) + jnp.exp(sink - m)
            outs.append(jnp.dot(p.astype(bf16), v_band, preferred_element_type=f32) / denom)
        o_ref[rows, :] = jnp.where(lane < HEAD_DIM, outs[0], outs[1]).astype(bf16)


def _swa(nat, sinks):
    s = nat.shape[0]
    ratio = SWA_TILE // SWA_WINDOW

    def prev_rows(t):
        return jnp.maximum(t * ratio - 1, 0)

    return pl.pallas_call(
        _swa_kernel,
        grid=(2, s // SWA_TILE),
        in_specs=[pl.BlockSpec(memory_space=pltpu.SMEM),
                  pl.BlockSpec((SWA_TILE, LANES), lambda p, t: (t, NAT_QB + p)),
                  pl.BlockSpec((SWA_TILE, LANES), lambda p, t: (t, NAT_KB + p)),
                  pl.BlockSpec((SWA_WINDOW, LANES), lambda p, t: (prev_rows(t), NAT_KB + p)),
                  pl.BlockSpec((SWA_TILE, LANES), lambda p, t: (t, NAT_VB + p)),
                  pl.BlockSpec((SWA_WINDOW, LANES), lambda p, t: (prev_rows(t), NAT_VB + p))],
        out_specs=pl.BlockSpec((SWA_TILE, LANES), lambda p, t: (t, p)),
        out_shape=jax.ShapeDtypeStruct((s, 2 * LANES), bf16),
        compiler_params=_params(2),
        name="swa_attention",
    )(sinks, nat, nat, nat, nat, nat)


def _diff_kernel(lam_init, qt_ref, k_ref, vt_ref, lq1_ref, lk1_ref, lq2_ref, lk2_ref, sub_ref,
                 o_ref, m_ref, acc_ref, sa_ref, mxa_ref, sb_ref, mxb_ref):
    t = pl.program_id(1)
    q_maps = _split_rows(qt_ref[...])

    own_group = t // GROUP_BLOCKS

    def block_rows(g, b):
        return pl.ds(pl.multiple_of((g * GROUP_BLOCKS + b) * MOBA_BLOCK, MOBA_BLOCK), MOBA_BLOCK)

    def past_score(g, b, h):
        return jnp.dot(k_ref[block_rows(g, b), :], q_maps[h], preferred_element_type=f32)

    def own_score(b, h):
        key, query = _positions((own_group * GROUP_BLOCKS + b) * MOBA_BLOCK, MOBA_BLOCK, t)
        return jnp.where(key <= query, past_score(own_group, b, h), NEG)

    def value(g, b, h):
        return vt_ref[:, block_rows(g, b)]

    _flash_pipeline(2, own_group, own_score, past_score, value,
                    (_Slot(sa_ref, mxa_ref), _Slot(sb_ref, mxb_ref)), (m_ref, acc_ref))

    lam = (jnp.exp(jnp.sum(lq1_ref[...] * lk1_ref[...], axis=-1, keepdims=True))
           - jnp.exp(jnp.sum(lq2_ref[...] * lk2_ref[...], axis=-1, keepdims=True)) + lam_init)
    maps = [acc_ref[h, :LANES, :] / acc_ref[h, LANES:LANES + 1, :] for h in range(2)]
    out_t = maps[0] - lam * maps[1]
    out = _rms(out_t.T, sub_ref[...]) * (1.0 - lam_init)
    o_ref[...] = out.astype(bf16)


def _diff(nat, q_t, v_t, lq1, lk1, lq2, lk2, subln, lam_init):
    s = nat.shape[0]
    vec = _const_spec((1, HEAD_DIM))
    return pl.pallas_call(
        functools.partial(_diff_kernel, lam_init),
        grid=(4, s // ATT_TILE),
        in_specs=[pl.BlockSpec((LANES, ATT_TILE), lambda h, t: (2 + h, t)),
                  pl.BlockSpec((s, LANES), lambda h, t: (0, NAT_KC + h)),
                  pl.BlockSpec((LANES, s), lambda h, t: (2 + h, 0)),
                  vec, vec, vec, vec, _const_spec((1, LANES))],
        out_specs=pl.BlockSpec((ATT_TILE, LANES), lambda h, t: (t, h)),
        out_shape=jax.ShapeDtypeStruct((s, 4 * LANES), bf16),
        scratch_shapes=_flash_scratch(2, LANES),
        compiler_params=_params(2),
        name="diff_attention",
    )(q_t, nat, v_t, lq1, lk1, lq2, lk2, subln)


def _merge_kernel(x_ref, g_ref, wgate_ref, oa_ref, ob_ref, oc_ref, pa_ref, pb_ref, pc_ref, wo_ref, o_ref):
    x = x_ref[...]
    h = _rms(x, g_ref[...]).astype(bf16)
    branches = ((oa_ref, pa_ref), (ob_ref, pb_ref), (oc_ref, pc_ref))
    merged = jnp.zeros_like(x)
    for i, (b_ref, p_ref) in enumerate(branches):
        logits = jnp.dot(h, wgate_ref[:, i * D_MODEL:(i + 1) * D_MODEL], preferred_element_type=f32)
        y = jnp.dot(b_ref[...], p_ref[...], preferred_element_type=f32)
        merged = merged + jax.nn.sigmoid(logits) * y
    o_ref[...] = x + jnp.dot(merged.astype(bf16), wo_ref[...], preferred_element_type=f32)


def _merge(x, gain, w_gate, oa, ob, oc, pa, pb, pc, w_out):
    s = x.shape[0]

    def rows(width):
        return pl.BlockSpec((ROW_TILE, width), lambda i: (i, 0))

    return pl.pallas_call(
        _merge_kernel,
        grid=(s // ROW_TILE,),
        in_specs=[rows(D_MODEL), _const_spec((1, D_MODEL)), _const_spec((D_MODEL, GATE_COLS)),
                  rows(oa.shape[1]), rows(ob.shape[1]), rows(oc.shape[1]),
                  _const_spec(pa.shape), _const_spec(pb.shape), _const_spec(pc.shape),
                  _const_spec((D_MODEL, D_MODEL))],
        out_specs=rows(D_MODEL),
        out_shape=jax.ShapeDtypeStruct(x.shape, f32),
        compiler_params=_params(1),
        name="gated_merge",
    )(x, gain, w_gate, oa, ob, oc, pa, pb, pc, w_out)


def _rope_tables(seq):
    pos = jnp.arange(seq, dtype=f32)
    inv_freq = ROPE_THETA ** (-jnp.arange(0, HEAD_DIM, 2, dtype=f32) / HEAD_DIM)
    ang = pos[:, None] * inv_freq[None, :]
    cos, sin = jnp.cos(ang), jnp.sin(ang)
    return (jnp.tile(cos, (1, 4)), jnp.concatenate([-sin, sin, -sin, sin], axis=-1), cos.T, sin.T)


def _layer_weights(w_in):
    qa, ka, va = w_in[:, 0:256], w_in[:, 256:512], w_in[:, 512:768]
    qb, kb, vb = w_in[:, 768:1024], w_in[:, 1024:1152], w_in[:, 1152:1280]
    qc, kc, vc = w_in[:, 1280:1792], w_in[:, 1792:2304], w_in[:, 2304:2816]
    gates = w_in[:, 2816:]

    def dup(w):
        h0, h1 = w[:, :HEAD_DIM], w[:, HEAD_DIM:]
        return jnp.concatenate([h0, h0, h1, h1], axis=1)

    w_nat = jnp.concatenate([ka, kc, qb, dup(kb), dup(vb)], axis=1)
    w_t = jnp.concatenate([qa, qc, va, vc], axis=1).T
    return w_nat.astype(bf16), w_t.astype(bf16), gates.astype(bf16)


def _head_gains(qa, ka, qb, kb, qc, kc):
    scale = HEAD_DIM ** -0.5
    scale2 = scale * math.log2(math.e)
    nat = jnp.concatenate([jnp.tile(ka, 4), jnp.tile(kc, 8), jnp.tile(qb * scale, 4), jnp.tile(kb, 4)])
    feat = jnp.concatenate([jnp.tile(qa * scale2, 4), jnp.tile(qc * scale2, 8)])
    return nat[None, :], feat[:, None]


def kernel(x, ffn1_norm, ffn1_w_gate, ffn1_w_up, ffn1_w_down, mix_norm, w_in, moba_q_norm, moba_k_norm, swa_q_norm, swa_k_norm, swa_sinks, diff_q_norm, diff_k_norm, diff_lambda_q1, diff_lambda_k1, diff_lambda_q2, diff_lambda_k2, diff_subln, w_branch_a, w_branch_b, w_branch_c, w_out, ffn2_norm, ffn2_w_gate, ffn2_w_up, ffn2_w_down):
    batch, seq, _ = x.shape
    assert batch == 1 and seq % SWA_TILE == 0 and seq % ROW_TILE == 0 and seq % KEY_GROUP == 0
    depth = w_in.shape[0]
    cos, sin, cos_t, sin_t = _rope_tables(seq)
    xs = x[0]
    for l in range(depth):
        lam_init = 0.8 - 0.6 * math.exp(-0.3 * l)
        xs = _ffn(xs, ffn1_norm[l][None], ffn1_w_gate[l].astype(bf16), ffn1_w_up[l].astype(bf16),
                  ffn1_w_down[l].astype(bf16))
        w_nat, w_t, w_gate = _layer_weights(w_in[l])
        hg_nat, hg_t = _head_gains(moba_q_norm[l], moba_k_norm[l], swa_q_norm[l], swa_k_norm[l],
                                   diff_q_norm[l], diff_k_norm[l])
        nat, q_t, v_t, kmean = _proj(xs, mix_norm[l][None], w_nat, w_t, hg_nat, hg_t, cos, sin, cos_t, sin_t)
        oa = _moba(nat, q_t, v_t, kmean.reshape(seq // MOBA_BLOCK, 2 * LANES))
        ob = _swa(nat, swa_sinks[l])
        oc = _diff(nat, q_t, v_t, diff_lambda_q1[l][None], diff_lambda_k1[l][None], diff_lambda_q2[l][None],
                   diff_lambda_k2[l][None], diff_subln[l][None], lam_init)
        xs = _merge(xs, mix_norm[l][None], w_gate, oa, ob, oc, w_branch_a[l].astype(bf16),
                    w_branch_b[l].astype(bf16), w_branch_c[l].astype(bf16), w_out[l].astype(bf16))
        xs = _ffn(xs, ffn2_norm[l][None], ffn2_w_gate[l].astype(bf16), ffn2_w_up[l].astype(bf16),
                  ffn2_w_down[l].astype(bf16))
    return xs[None]
```

```python
import functools
import math

import jax
import jax.numpy as jnp
from jax import lax
from jax.experimental import pallas as pl
from jax.experimental.pallas import tpu as pltpu

D_MODEL = 1024
D_FF = 2816
HEAD_DIM = 64
HALF = HEAD_DIM // 2
LANES = 128
MOBA_BLOCK = 256
MOBA_TOPK = 3
SWA_WINDOW = 128
N_BRANCH = 3
ROPE_THETA = 10000.0
EPS = 1e-6
NEG = -1e30
FFN_HALF = 0.5

NAT_QK_COLS = 1280
NAT_COLS = 1536
NAT_KA, NAT_KC, NAT_QB, NAT_KB, NAT_VB = 0, 2, 6, 8, 10
QT_ROWS = 768
VT_ROWS = 768
GATE_COLS = N_BRANCH * D_MODEL

ROW_TILE = 512
ATT_TILE = 256
GROUP_BLOCKS = 4
KEY_GROUP = GROUP_BLOCKS * MOBA_BLOCK
STAGES_PER_TRIP = 4
PARK_LEAD = 1
SUM_ROWS = 16
SWA_TILE = 512
VMEM_LIMIT = 48 * 1024 * 1024

f32 = jnp.float32
bf16 = jnp.bfloat16

_NT = (((1,), (1,)), ((), ()))


def _rms(x, gain):
    return x * lax.rsqrt(jnp.mean(x * x, axis=-1, keepdims=True) + EPS) * gain


def _params(n_axes):
    return pltpu.CompilerParams(
        dimension_semantics=("arbitrary",) * n_axes, vmem_limit_bytes=VMEM_LIMIT)


def _const_spec(shape):
    return pl.BlockSpec(shape, lambda *_: (0,) * len(shape), pipeline_mode=pl.Buffered(1))


def _ffn_kernel(x_ref, g_ref, wg_ref, wu_ref, wd_ref, o_ref):
    x = x_ref[...]
    h = _rms(x, g_ref[...]).astype(bf16)
    a = jnp.dot(h, wg_ref[...], preferred_element_type=f32)
    b = jnp.dot(h, wu_ref[...], preferred_element_type=f32)
    act = (a * jax.nn.sigmoid(a) * b).astype(bf16)
    o_ref[...] = x + FFN_HALF * jnp.dot(act, wd_ref[...], preferred_element_type=f32)


def _ffn(x, gain, wg, wu, wd):
    s = x.shape[0]
    row = pl.BlockSpec((ROW_TILE, D_MODEL), lambda i: (i, 0))
    return pl.pallas_call(
        _ffn_kernel,
        grid=(s // ROW_TILE,),
        in_specs=[row, _const_spec((1, D_MODEL)), _const_spec((D_MODEL, D_FF)),
                  _const_spec((D_MODEL, D_FF)), _const_spec((D_FF, D_MODEL))],
        out_specs=row,
        out_shape=jax.ShapeDtypeStruct(x.shape, f32),
        compiler_params=_params(1),
        name="ffn",
    )(x, gain, wg, wu, wd)


def _proj_kernel(x_ref, g_ref, wn_ref, wt_ref, hg_ref, hgt_ref, cos_ref, sin_ref, cost_ref, sint_ref,
                 nat_ref, qt_ref, vt_ref, km_ref):
    x = x_ref[...]
    h = _rms(x, g_ref[...]).astype(bf16)

    proj = jnp.dot(h, wn_ref[...], preferred_element_type=f32)
    lane = lax.broadcasted_iota(jnp.int32, (1, LANES), 1)
    first_head = lane < HEAD_DIM
    first_half = (lane % HEAD_DIM) < HALF
    cos = cos_ref[...]
    sin = sin_ref[...]
    for c in range(NAT_QK_COLS // LANES):
        cols = slice(c * LANES, (c + 1) * LANES)
        y = proj[:, cols]
        sq = y * y
        ss0 = jnp.sum(jnp.where(first_head, sq, 0.0), axis=-1, keepdims=True)
        ss1 = jnp.sum(jnp.where(first_head, 0.0, sq), axis=-1, keepdims=True)
        inv = jnp.where(first_head, lax.rsqrt(ss0 / HEAD_DIM + EPS), lax.rsqrt(ss1 / HEAD_DIM + EPS))
        y = y * inv * hg_ref[:, cols]
        partner = jnp.where(first_half, pltpu.roll(y, LANES - HALF, 1), pltpu.roll(y, HALF, 1))
        y = y * cos + partner * sin
        nat_ref[:, cols] = y.astype(bf16)
        if c in (NAT_KA, NAT_KA + 1):
            for b in range(ROW_TILE // MOBA_BLOCK):
                blk = y[b * MOBA_BLOCK:(b + 1) * MOBA_BLOCK]
                km_ref[b, :, (c - NAT_KA) * LANES:(c - NAT_KA + 1) * LANES] = jnp.mean(blk, axis=0, keepdims=True)
    nat_ref[:, NAT_QK_COLS:] = proj[:, NAT_QK_COLS:].astype(bf16)

    proj_t = lax.dot_general(wt_ref[...], h, _NT, preferred_element_type=f32)
    cos_t = cost_ref[...]
    sin_t = sint_ref[...]
    for c in range(QT_ROWS // HEAD_DIM):
        rows = slice(c * HEAD_DIM, (c + 1) * HEAD_DIM)
        y = proj_t[rows, :]
        inv = lax.rsqrt(jnp.mean(y * y, axis=0, keepdims=True) + EPS)
        y = y * inv * hgt_ref[rows, :]
        y1, y2 = y[:HALF], y[HALF:]
        qt_ref[c * HEAD_DIM:c * HEAD_DIM + HALF, :] = (y1 * cos_t - y2 * sin_t).astype(bf16)
        qt_ref[c * HEAD_DIM + HALF:(c + 1) * HEAD_DIM, :] = (y2 * cos_t + y1 * sin_t).astype(bf16)
    vt_ref[...] = proj_t[QT_ROWS:, :].astype(bf16)


def _proj(x, gain, w_nat, w_t, hg_nat, hg_t, cos, sin, cos_t, sin_t):
    s = x.shape[0]
    nb = ROW_TILE // MOBA_BLOCK
    return pl.pallas_call(
        _proj_kernel,
        grid=(s // ROW_TILE,),
        in_specs=[pl.BlockSpec((ROW_TILE, D_MODEL), lambda i: (i, 0)),
                  _const_spec((1, D_MODEL)),
                  _const_spec((D_MODEL, NAT_COLS)),
                  _const_spec((QT_ROWS + VT_ROWS, D_MODEL)),
                  _const_spec((1, NAT_QK_COLS)),
                  _const_spec((QT_ROWS, 1)),
                  pl.BlockSpec((ROW_TILE, LANES), lambda i: (i, 0)),
                  pl.BlockSpec((ROW_TILE, LANES), lambda i: (i, 0)),
                  pl.BlockSpec((HALF, ROW_TILE), lambda i: (0, i)),
                  pl.BlockSpec((HALF, ROW_TILE), lambda i: (0, i))],
        out_specs=[pl.BlockSpec((ROW_TILE, NAT_COLS), lambda i: (i, 0)),
                   pl.BlockSpec((QT_ROWS, ROW_TILE), lambda i: (0, i)),
                   pl.BlockSpec((VT_ROWS, ROW_TILE), lambda i: (0, i)),
                   pl.BlockSpec((nb, 1, 2 * LANES), lambda i: (i, 0, 0))],
        out_shape=[jax.ShapeDtypeStruct((s, NAT_COLS), bf16),
                   jax.ShapeDtypeStruct((QT_ROWS, s), bf16),
                   jax.ShapeDtypeStruct((VT_ROWS, s), bf16),
                   jax.ShapeDtypeStruct((s // MOBA_BLOCK, 1, 2 * LANES), f32)],
        compiler_params=_params(1),
        name="qkv_proj",
    )(x, gain, w_nat, w_t, hg_nat, hg_t, cos, sin, cos_t, sin_t)


def _split_rows(q_t):
    zero = jnp.zeros((HEAD_DIM, q_t.shape[1]), q_t.dtype)
    return (jnp.concatenate([q_t[:HEAD_DIM], zero], axis=0),
            jnp.concatenate([zero, q_t[HEAD_DIM:]], axis=0))


class _Slot:
    def __init__(self, s_ref, mx_ref):
        self.s, self.mx = s_ref, mx_ref


def _with_ones(v_t):
    return jnp.concatenate([v_t, jnp.ones((SUM_ROWS, v_t.shape[1]), v_t.dtype)], axis=0)


def _flash_stage(n_maps, state, score=None, nxt=None, cur=None, value=None):
    m_ref, acc_ref = state
    maps = range(n_maps)
    if cur is not None:
        m_old = [m_ref[h] for h in maps]
        m_new = [jnp.maximum(m_old[h], cur.mx[h]) for h in maps]
        for h in maps:
            acc_ref[h] = jnp.exp2(m_old[h] - m_new[h]) * acc_ref[h]
            m_ref[h] = m_new[h]
    col_max = [None] * n_maps

    def park(b):
        rows = slice(b * MOBA_BLOCK, (b + 1) * MOBA_BLOCK)
        for h in maps:
            s_b = score(b, h)
            nxt.s[h, rows, :] = s_b
            top = jnp.max(s_b, axis=0, keepdims=True)
            col_max[h] = top if b == 0 else jnp.maximum(col_max[h], top)
        if b == GROUP_BLOCKS - 1:
            for h in maps:
                nxt.mx[h] = col_max[h]

    def consume(b):
        rows = slice(b * MOBA_BLOCK, (b + 1) * MOBA_BLOCK)
        for h in maps:
            p = jnp.exp2(cur.s[h, rows, :] - m_new[h]).astype(bf16)
            acc_ref[h] += jnp.dot(_with_ones(value(b, h)), p, preferred_element_type=f32)

    if cur is not None and nxt is cur:
        for b in range(GROUP_BLOCKS):
            consume(b)
            park(b)
        return
    for step in range(GROUP_BLOCKS + PARK_LEAD):
        if score is not None and step < GROUP_BLOCKS:
            park(step)
        if cur is not None and step >= PARK_LEAD:
            consume(step - PARK_LEAD)


def _flash_pipeline(n_maps, first_tile, own_group, own_score, past_score, next_own_score, value, slots, state):
    m_ref, acc_ref = state
    m_ref[...] = jnp.full(m_ref.shape, NEG, f32)
    acc_ref[...] = jnp.zeros(acc_ref.shape, f32)

    def stage(base, offset, last):
        i = base + offset
        cur = slots[offset % 2]
        nxt = slots[0] if last else slots[1 - offset % 2]
        group = jnp.where(i == 0, own_group, i - 1)
        score = next_own_score if last else (lambda b, h: past_score(i, b, h))
        _flash_stage(n_maps, state, score=score, nxt=nxt, cur=cur, value=lambda b, h: value(group, b, h))

    pl.when(first_tile)(functools.partial(_flash_stage, n_maps, state, score=own_score, nxt=slots[0]))

    def trip(k, carry):
        for offset in range(STAGES_PER_TRIP):
            stage(k * STAGES_PER_TRIP, offset, False)
        return carry

    full_trips = own_group // STAGES_PER_TRIP
    lax.fori_loop(0, full_trips, trip, 0)

    base = full_trips * STAGES_PER_TRIP
    left = own_group - base
    for offset in range(STAGES_PER_TRIP):
        if offset > 0:
            pl.when(left >= offset)(functools.partial(stage, base, offset - 1, False))
        pl.when(left == offset)(functools.partial(stage, base, offset, True))


def _flash_scratch(n_maps, dv):
    row = pltpu.VMEM((n_maps, 1, ATT_TILE), f32)
    scores = pltpu.VMEM((n_maps, KEY_GROUP, ATT_TILE), f32)
    return [row, pltpu.VMEM((n_maps, dv + SUM_ROWS, ATT_TILE), f32), scores, row, scores, row]


def _positions(first_key, n_keys, t):
    key = first_key + lax.broadcasted_iota(jnp.int32, (n_keys, ATT_TILE), 0)
    query = t * ATT_TILE + lax.broadcasted_iota(jnp.int32, (n_keys, ATT_TILE), 1)
    return key, query


def _moba_kernel(qt_ref, qn_ref, k_ref, vt_ref, km_ref, o_ref, m_ref, acc_ref, sa_ref, mxa_ref, sb_ref, mxb_ref,
                 picks_ref):
    t = pl.program_id(1)
    t_next = jnp.minimum(t + 1, pl.num_programs(1) - 1)
    n_blocks = km_ref.shape[0]

    km = km_ref[...]
    km_hi = km.astype(bf16)
    rem = km - km_hi.astype(f32)
    km_mid = rem.astype(bf16)
    km_lo = (rem - km_mid.astype(f32)).astype(bf16)
    blk = lax.broadcasted_iota(jnp.int32, (n_blocks, 1), 0).astype(f32)

    def block_rows(g, b):
        return pl.ds(pl.multiple_of((g * GROUP_BLOCKS + b) * MOBA_BLOCK, MOBA_BLOCK), MOBA_BLOCK)

    def pick_blocks(tile, q_heads):
        tile_f = tile.astype(f32)
        picks = []
        for qh in q_heads:
            gate = (jnp.dot(km_hi, qh, preferred_element_type=f32)
                    + jnp.dot(km_mid, qh, preferred_element_type=f32)
                    + jnp.dot(km_lo, qh, preferred_element_type=f32))
            gate = jnp.where(blk < tile_f, gate, NEG)
            chosen = []
            for _ in range(MOBA_TOPK):
                best = jnp.max(gate, axis=0, keepdims=True)
                idx = jnp.min(jnp.where(gate == best, blk, 1e9), axis=0, keepdims=True)
                chosen.append(jnp.where(best > 0.5 * NEG, idx, -1.0))
                gate = jnp.where(blk == idx, 2.0 * NEG, gate)
            picks.append(chosen)
        return picks

    def tile_scores(tile, q_heads, picks):
        def past_score(g, b, h):
            s_b = jnp.dot(k_ref[block_rows(g, b), :], q_heads[h], preferred_element_type=f32)
            blk_f = (g * GROUP_BLOCKS + b).astype(f32)
            hit = (picks[h][0] == blk_f) | (picks[h][1] == blk_f) | (picks[h][2] == blk_f)
            return s_b, jnp.where(hit, s_b, NEG)

        def own_score(b, h):
            group = tile // GROUP_BLOCKS
            s_b, masked = past_score(group, b, h)
            key, query = _positions((group * GROUP_BLOCKS + b) * MOBA_BLOCK, MOBA_BLOCK, tile)
            return jnp.where(key >= tile * ATT_TILE, jnp.where(key <= query, s_b, NEG), masked)

        return own_score, lambda g, b, h: past_score(g, b, h)[1]

    def value(g, b, h):
        return vt_ref[h * HEAD_DIM:(h + 1) * HEAD_DIM, block_rows(g, b)]

    @pl.when(t == 0)
    def _():
        picks_ref[...] = jnp.full(picks_ref.shape, -1.0, f32)

    picks = [[picks_ref[h * MOBA_TOPK + i:h * MOBA_TOPK + i + 1, :] for i in range(MOBA_TOPK)] for h in range(2)]
    q_next = _split_rows(qn_ref[...])
    picks_next = pick_blocks(t_next, q_next)
    own_score, past_score = tile_scores(t, _split_rows(qt_ref[...]), picks)
    next_own_score, _ = tile_scores(t_next, q_next, picks_next)
    _flash_pipeline(2, t == 0, t // GROUP_BLOCKS, own_score, past_score, next_own_score, value,
                    (_Slot(sa_ref, mxa_ref), _Slot(sb_ref, mxb_ref)), (m_ref, acc_ref))
    for h in range(2):
        for i in range(MOBA_TOPK):
            picks_ref[h * MOBA_TOPK + i:h * MOBA_TOPK + i + 1, :] = picks_next[h][i]

    heads = [acc_ref[h, :HEAD_DIM, :] / acc_ref[h, HEAD_DIM:HEAD_DIM + 1, :] for h in range(2)]
    o_ref[...] = jnp.concatenate(heads, axis=0).T.astype(bf16)


def _moba(nat, q_t, v_t, kmean):
    s = nat.shape[0]
    n_blocks = s // MOBA_BLOCK
    last = s // ATT_TILE - 1
    return pl.pallas_call(
        _moba_kernel,
        grid=(2, s // ATT_TILE),
        in_specs=[pl.BlockSpec((LANES, ATT_TILE), lambda p, t: (p, t)),
                  pl.BlockSpec((LANES, ATT_TILE), lambda p, t: (p, jnp.minimum(t + 1, last))),
                  pl.BlockSpec((s, LANES), lambda p, t: (0, NAT_KA + p)),
                  pl.BlockSpec((LANES, s), lambda p, t: (p, 0)),
                  pl.BlockSpec((n_blocks, LANES), lambda p, t: (0, p))],
        out_specs=pl.BlockSpec((ATT_TILE, LANES), lambda p, t: (t, p)),
        out_shape=jax.ShapeDtypeStruct((s, 2 * LANES), bf16),
        scratch_shapes=_flash_scratch(2, HEAD_DIM) + [pltpu.VMEM((2 * MOBA_TOPK, ATT_TILE), f32)],
        compiler_params=_params(2),
        name="moba_attention",
    )(q_t, q_t, nat, v_t, kmean)


def _swa_kernel(sink_ref, q_ref, k_ref, kp_ref, v_ref, vp_ref, o_ref):
    p_id = pl.program_id(0)
    t = pl.program_id(1)
    sub_tiles = SWA_TILE // SWA_WINDOW
    lane = lax.broadcasted_iota(jnp.int32, (1, LANES), 1)
    zero = jnp.zeros((SWA_WINDOW, LANES), bf16)
    qi = lax.broadcasted_iota(jnp.int32, (SWA_WINDOW, 2 * SWA_WINDOW), 0)
    kj = lax.broadcasted_iota(jnp.int32, (SWA_WINDOW, 2 * SWA_WINDOW), 1)
    back = qi + SWA_WINDOW - kj
    in_window = (back >= 0) & (back < SWA_WINDOW)
    for sub in range(sub_tiles):
        rows = slice(sub * SWA_WINDOW, (sub + 1) * SWA_WINDOW)
        if sub == 0:
            k_prev, v_prev = kp_ref[...], vp_ref[...]
            mask = in_window & (kj >= SWA_WINDOW - t * SWA_TILE)
        else:
            prev = slice((sub - 1) * SWA_WINDOW, sub * SWA_WINDOW)
            k_prev, v_prev = k_ref[prev, :], v_ref[prev, :]
            mask = in_window
        k_band = jnp.concatenate([k_prev, k_ref[rows, :]], axis=0)
        v_band = jnp.concatenate([v_prev, v_ref[rows, :]], axis=0)
        q = q_ref[rows, :]
        outs = []
        for h, qh in enumerate((jnp.where(lane < HEAD_DIM, q, zero), jnp.where(lane < HEAD_DIM, zero, q))):
            sink = sink_ref[2 * p_id + h]
            s = lax.dot_general(qh, k_band, _NT, preferred_element_type=f32)
            s = jnp.where(mask, s, NEG)
            m = jnp.maximum(jnp.max(s, axis=-1, keepdims=True), sink)
            p = jnp.exp(s - m)
            denom = jnp.sum(p, axis=-1, keepdims=True) + jnp.exp(sink - m)
            outs.append(jnp.dot(p.astype(bf16), v_band, preferred_element_type=f32) / denom)
        o_ref[rows, :] = jnp.where(lane < HEAD_DIM, outs[0], outs[1]).astype(bf16)


def _swa(nat, sinks):
    s = nat.shape[0]
    ratio = SWA_TILE // SWA_WINDOW

    def prev_rows(t):
        return jnp.maximum(t * ratio - 1, 0)

    return pl.pallas_call(
        _swa_kernel,
        grid=(2, s // SWA_TILE),
        in_specs=[pl.BlockSpec(memory_space=pltpu.SMEM),
                  pl.BlockSpec((SWA_TILE, LANES), lambda p, t: (t, NAT_QB + p)),
                  pl.BlockSpec((SWA_TILE, LANES), lambda p, t: (t, NAT_KB + p)),
                  pl.BlockSpec((SWA_WINDOW, LANES), lambda p, t: (prev_rows(t), NAT_KB + p)),
                  pl.BlockSpec((SWA_TILE, LANES), lambda p, t: (t, NAT_VB + p)),
                  pl.BlockSpec((SWA_WINDOW, LANES), lambda p, t: (prev_rows(t), NAT_VB + p))],
        out_specs=pl.BlockSpec((SWA_TILE, LANES), lambda p, t: (t, p)),
        out_shape=jax.ShapeDtypeStruct((s, 2 * LANES), bf16),
        compiler_params=_params(2),
        name="swa_attention",
    )(sinks, nat, nat, nat, nat, nat)


def _diff_kernel(lam_init, qt_ref, qn_ref, k_ref, vt_ref, lq1_ref, lk1_ref, lq2_ref, lk2_ref, sub_ref,
                 o_ref, m_ref, acc_ref, sa_ref, mxa_ref, sb_ref, mxb_ref):
    t = pl.program_id(1)
    t_next = jnp.minimum(t + 1, pl.num_programs(1) - 1)

    def block_rows(g, b):
        return pl.ds(pl.multiple_of((g * GROUP_BLOCKS + b) * MOBA_BLOCK, MOBA_BLOCK), MOBA_BLOCK)

    def tile_scores(tile, q_ref):
        q_maps = _split_rows(q_ref[...])

        def past_score(g, b, h):
            return jnp.dot(k_ref[block_rows(g, b), :], q_maps[h], preferred_element_type=f32)

        def own_score(b, h):
            group = tile // GROUP_BLOCKS
            key, query = _positions((group * GROUP_BLOCKS + b) * MOBA_BLOCK, MOBA_BLOCK, tile)
            return jnp.where(key <= query, past_score(group, b, h), NEG)

        return own_score, past_score

    def value(g, b, h):
        return vt_ref[:, block_rows(g, b)]

    own_score, past_score = tile_scores(t, qt_ref)
    next_own_score, _ = tile_scores(t_next, qn_ref)
    _flash_pipeline(2, t == 0, t // GROUP_BLOCKS, own_score, past_score, next_own_score, value,
                    (_Slot(sa_ref, mxa_ref), _Slot(sb_ref, mxb_ref)), (m_ref, acc_ref))

    lam = (jnp.exp(jnp.sum(lq1_ref[...] * lk1_ref[...], axis=-1, keepdims=True))
           - jnp.exp(jnp.sum(lq2_ref[...] * lk2_ref[...], axis=-1, keepdims=True)) + lam_init)
    maps = [acc_ref[h, :LANES, :] / acc_ref[h, LANES:LANES + 1, :] for h in range(2)]
    out_t = maps[0] - lam * maps[1]
    out = _rms(out_t.T, sub_ref[...]) * (1.0 - lam_init)
    o_ref[...] = out.astype(bf16)


def _diff(nat, q_t, v_t, lq1, lk1, lq2, lk2, subln, lam_init):
    s = nat.shape[0]
    vec = _const_spec((1, HEAD_DIM))
    last = s // ATT_TILE - 1
    return pl.pallas_call(
        functools.partial(_diff_kernel, lam_init),
        grid=(4, s // ATT_TILE),
        in_specs=[pl.BlockSpec((LANES, ATT_TILE), lambda h, t: (2 + h, t)),
                  pl.BlockSpec((LANES, ATT_TILE), lambda h, t: (2 + h, jnp.minimum(t + 1, last))),
                  pl.BlockSpec((s, LANES), lambda h, t: (0, NAT_KC + h)),
                  pl.BlockSpec((LANES, s), lambda h, t: (2 + h, 0)),
                  vec, vec, vec, vec, _const_spec((1, LANES))],
        out_specs=pl.BlockSpec((ATT_TILE, LANES), lambda h, t: (t, h)),
        out_shape=jax.ShapeDtypeStruct((s, 4 * LANES), bf16),
        scratch_shapes=_flash_scratch(2, LANES),
        compiler_params=_params(2),
        name="diff_attention",
    )(q_t, q_t, nat, v_t, lq1, lk1, lq2, lk2, subln)


def _merge_kernel(x_ref, g_ref, wgate_ref, oa_ref, ob_ref, oc_ref, pa_ref, pb_ref, pc_ref, wo_ref, o_ref):
    x = x_ref[...]
    h = _rms(x, g_ref[...]).astype(bf16)
    branches = ((oa_ref, pa_ref), (ob_ref, pb_ref), (oc_ref, pc_ref))
    merged = jnp.zeros_like(x)
    for i, (b_ref, p_ref) in enumerate(branches):
        logits = jnp.dot(h, wgate_ref[:, i * D_MODEL:(i + 1) * D_MODEL], preferred_element_type=f32)
        y = jnp.dot(b_ref[...], p_ref[...], preferred_element_type=f32)
        merged = merged + jax.nn.sigmoid(logits) * y
    o_ref[...] = x + jnp.dot(merged.astype(bf16), wo_ref[...], preferred_element_type=f32)


def _merge(x, gain, w_gate, oa, ob, oc, pa, pb, pc, w_out):
    s = x.shape[0]

    def rows(width):
        return pl.BlockSpec((ROW_TILE, width), lambda i: (i, 0))

    return pl.pallas_call(
        _merge_kernel,
        grid=(s // ROW_TILE,),
        in_specs=[rows(D_MODEL), _const_spec((1, D_MODEL)), _const_spec((D_MODEL, GATE_COLS)),
                  rows(oa.shape[1]), rows(ob.shape[1]), rows(oc.shape[1]),
                  _const_spec(pa.shape), _const_spec(pb.shape), _const_spec(pc.shape),
                  _const_spec((D_MODEL, D_MODEL))],
        out_specs=rows(D_MODEL),
        out_shape=jax.ShapeDtypeStruct(x.shape, f32),
        compiler_params=_params(1),
        name="gated_merge",
    )(x, gain, w_gate, oa, ob, oc, pa, pb, pc, w_out)


def _rope_tables(seq):
    pos = jnp.arange(seq, dtype=f32)
    inv_freq = ROPE_THETA ** (-jnp.arange(0, HEAD_DIM, 2, dtype=f32) / HEAD_DIM)
    ang = pos[:, None] * inv_freq[None, :]
    cos, sin = jnp.cos(ang), jnp.sin(ang)
    return (jnp.tile(cos, (1, 4)), jnp.concatenate([-sin, sin, -sin, sin], axis=-1), cos.T, sin.T)


def _layer_weights(w_in):
    qa, ka, va = w_in[:, 0:256], w_in[:, 256:512], w_in[:, 512:768]
    qb, kb, vb = w_in[:, 768:1024], w_in[:, 1024:1152], w_in[:, 1152:1280]
    qc, kc, vc = w_in[:, 1280:1792], w_in[:, 1792:2304], w_in[:, 2304:2816]
    gates = w_in[:, 2816:]

    def dup(w):
        h0, h1 = w[:, :HEAD_DIM], w[:, HEAD_DIM:]
        return jnp.concatenate([h0, h0, h1, h1], axis=1)

    w_nat = jnp.concatenate([ka, kc, qb, dup(kb), dup(vb)], axis=1)
    w_t = jnp.concatenate([qa, qc, va, vc], axis=1).T
    return w_nat.astype(bf16), w_t.astype(bf16), gates.astype(bf16)


def _head_gains(qa, ka, qb, kb, qc, kc):
    scale = HEAD_DIM ** -0.5
    scale2 = scale * math.log2(math.e)
    nat = jnp.concatenate([jnp.tile(ka, 4), jnp.tile(kc, 8), jnp.tile(qb * scale, 4), jnp.tile(kb, 4)])
    feat = jnp.concatenate([jnp.tile(qa * scale2, 4), jnp.tile(qc * scale2, 8)])
    return nat[None, :], feat[:, None]


def kernel(x, ffn1_norm, ffn1_w_gate, ffn1_w_up, ffn1_w_down, mix_norm, w_in, moba_q_norm, moba_k_norm, swa_q_norm, swa_k_norm, swa_sinks, diff_q_norm, diff_k_norm, diff_lambda_q1, diff_lambda_k1, diff_lambda_q2, diff_lambda_k2, diff_subln, w_branch_a, w_branch_b, w_branch_c, w_out, ffn2_norm, ffn2_w_gate, ffn2_w_up, ffn2_w_down):
    batch, seq, _ = x.shape
    assert batch == 1 and seq % SWA_TILE == 0 and seq % ROW_TILE == 0 and seq % KEY_GROUP == 0
    depth = w_in.shape[0]
    cos, sin, cos_t, sin_t = _rope_tables(seq)
    xs = x[0]
    for l in range(depth):
        lam_init = 0.8 - 0.6 * math.exp(-0.3 * l)
        xs = _ffn(xs, ffn1_norm[l][None], ffn1_w_gate[l].astype(bf16), ffn1_w_up[l].astype(bf16),
                  ffn1_w_down[l].astype(bf16))
        w_nat, w_t, w_gate = _layer_weights(w_in[l])
        hg_nat, hg_t = _head_gains(moba_q_norm[l], moba_k_norm[l], swa_q_norm[l], swa_k_norm[l],
                                   diff_q_norm[l], diff_k_norm[l])
        nat, q_t, v_t, kmean = _proj(xs, mix_norm[l][None], w_nat, w_t, hg_nat, hg_t, cos, sin, cos_t, sin_t)
        oa = _moba(nat, q_t, v_t, kmean.reshape(seq // MOBA_BLOCK, 2 * LANES))
        ob = _swa(nat, swa_sinks[l])
        oc = _diff(nat, q_t, v_t, diff_lambda_q1[l][None], diff_lambda_k1[l][None], diff_lambda_q2[l][None],
                   diff_lambda_k2[l][None], diff_subln[l][None], lam_init)
        xs = _merge(xs, mix_norm[l][None], w_gate, oa, ob, oc, w_branch_a[l].astype(bf16),
                    w_branch_b[l].astype(bf16), w_branch_c[l].astype(bf16), w_out[l].astype(bf16))
        xs = _ffn(xs, ffn2_norm[l][None], ffn2_w_gate[l].astype(bf16), ffn2_w_up[l].astype(bf16),
                  ffn2_w_down[l].astype(bf16))
    return xs[None]
```

```python
import functools
import math

import jax
import jax.numpy as jnp
from jax import lax
from jax.experimental import pallas as pl
from jax.experimental.pallas import tpu as pltpu

D_MODEL = 1024
D_FF = 2816
HEAD_DIM = 64
HALF = HEAD_DIM // 2
LANES = 128
MOBA_BLOCK = 256
MOBA_TOPK = 3
SWA_WINDOW = 128
N_BRANCH = 3
ROPE_THETA = 10000.0
EPS = 1e-6
NEG = -1e30
FFN_HALF = 0.5

NAT_QK_COLS = 1280
NAT_COLS = 1536
NAT_KA, NAT_KC, NAT_QB, NAT_KB, NAT_VB = 0, 2, 6, 8, 10
QT_ROWS = 768
VT_ROWS = 768
GATE_COLS = N_BRANCH * D_MODEL

ROW_TILE = 512
ATT_TILE = 256
DIFF_TILE = 512
GROUP_BLOCKS = 4
KEY_GROUP = GROUP_BLOCKS * MOBA_BLOCK
STAGES_PER_TRIP = 4
PARK_LEAD = 1
SUM_ROWS = 16
SWA_TILE = 512
VMEM_LIMIT = 48 * 1024 * 1024

f32 = jnp.float32
bf16 = jnp.bfloat16

_NT = (((1,), (1,)), ((), ()))


def _rms(x, gain):
    return x * lax.rsqrt(jnp.mean(x * x, axis=-1, keepdims=True) + EPS) * gain


def _params(n_axes):
    return pltpu.CompilerParams(
        dimension_semantics=("arbitrary",) * n_axes, vmem_limit_bytes=VMEM_LIMIT)


def _const_spec(shape):
    return pl.BlockSpec(shape, lambda *_: (0,) * len(shape), pipeline_mode=pl.Buffered(1))


def _ffn_kernel(x_ref, g_ref, wg_ref, wu_ref, wd_ref, o_ref):
    x = x_ref[...]
    h = _rms(x, g_ref[...]).astype(bf16)
    a = jnp.dot(h, wg_ref[...], preferred_element_type=f32)
    b = jnp.dot(h, wu_ref[...], preferred_element_type=f32)
    act = (a * jax.nn.sigmoid(a) * b).astype(bf16)
    o_ref[...] = x + FFN_HALF * jnp.dot(act, wd_ref[...], preferred_element_type=f32)


def _ffn(x, gain, wg, wu, wd):
    s = x.shape[0]
    row = pl.BlockSpec((ROW_TILE, D_MODEL), lambda i: (i, 0))
    return pl.pallas_call(
        _ffn_kernel,
        grid=(s // ROW_TILE,),
        in_specs=[row, _const_spec((1, D_MODEL)), _const_spec((D_MODEL, D_FF)),
                  _const_spec((D_MODEL, D_FF)), _const_spec((D_FF, D_MODEL))],
        out_specs=row,
        out_shape=jax.ShapeDtypeStruct(x.shape, f32),
        compiler_params=_params(1),
        name="ffn",
    )(x, gain, wg, wu, wd)


def _proj_kernel(x_ref, g_ref, wn_ref, wt_ref, hg_ref, hgt_ref, cos_ref, sin_ref, cost_ref, sint_ref,
                 nat_ref, qt_ref, vt_ref, km_ref):
    x = x_ref[...]
    h = _rms(x, g_ref[...]).astype(bf16)

    proj = jnp.dot(h, wn_ref[...], preferred_element_type=f32)
    lane = lax.broadcasted_iota(jnp.int32, (1, LANES), 1)
    first_head = lane < HEAD_DIM
    first_half = (lane % HEAD_DIM) < HALF
    cos = cos_ref[...]
    sin = sin_ref[...]
    for c in range(NAT_QK_COLS // LANES):
        cols = slice(c * LANES, (c + 1) * LANES)
        y = proj[:, cols]
        sq = y * y
        ss0 = jnp.sum(jnp.where(first_head, sq, 0.0), axis=-1, keepdims=True)
        ss1 = jnp.sum(jnp.where(first_head, 0.0, sq), axis=-1, keepdims=True)
        inv = jnp.where(first_head, lax.rsqrt(ss0 / HEAD_DIM + EPS), lax.rsqrt(ss1 / HEAD_DIM + EPS))
        y = y * inv * hg_ref[:, cols]
        partner = jnp.where(first_half, pltpu.roll(y, LANES - HALF, 1), pltpu.roll(y, HALF, 1))
        y = y * cos + partner * sin
        nat_ref[:, cols] = y.astype(bf16)
        if c in (NAT_KA, NAT_KA + 1):
            for b in range(ROW_TILE // MOBA_BLOCK):
                blk = y[b * MOBA_BLOCK:(b + 1) * MOBA_BLOCK]
                km_ref[b, :, (c - NAT_KA) * LANES:(c - NAT_KA + 1) * LANES] = jnp.mean(blk, axis=0, keepdims=True)
    nat_ref[:, NAT_QK_COLS:] = proj[:, NAT_QK_COLS:].astype(bf16)

    proj_t = lax.dot_general(wt_ref[...], h, _NT, preferred_element_type=f32)
    cos_t = cost_ref[...]
    sin_t = sint_ref[...]
    for c in range(QT_ROWS // HEAD_DIM):
        rows = slice(c * HEAD_DIM, (c + 1) * HEAD_DIM)
        y = proj_t[rows, :]
        inv = lax.rsqrt(jnp.mean(y * y, axis=0, keepdims=True) + EPS)
        y = y * inv * hgt_ref[rows, :]
        y1, y2 = y[:HALF], y[HALF:]
        qt_ref[c * HEAD_DIM:c * HEAD_DIM + HALF, :] = (y1 * cos_t - y2 * sin_t).astype(bf16)
        qt_ref[c * HEAD_DIM + HALF:(c + 1) * HEAD_DIM, :] = (y2 * cos_t + y1 * sin_t).astype(bf16)
    vt_ref[...] = proj_t[QT_ROWS:, :].astype(bf16)


def _proj(x, gain, w_nat, w_t, hg_nat, hg_t, cos, sin, cos_t, sin_t):
    s = x.shape[0]
    nb = ROW_TILE // MOBA_BLOCK
    return pl.pallas_call(
        _proj_kernel,
        grid=(s // ROW_TILE,),
        in_specs=[pl.BlockSpec((ROW_TILE, D_MODEL), lambda i: (i, 0)),
                  _const_spec((1, D_MODEL)),
                  _const_spec((D_MODEL, NAT_COLS)),
                  _const_spec((QT_ROWS + VT_ROWS, D_MODEL)),
                  _const_spec((1, NAT_QK_COLS)),
                  _const_spec((QT_ROWS, 1)),
                  pl.BlockSpec((ROW_TILE, LANES), lambda i: (i, 0)),
                  pl.BlockSpec((ROW_TILE, LANES), lambda i: (i, 0)),
                  pl.BlockSpec((HALF, ROW_TILE), lambda i: (0, i)),
                  pl.BlockSpec((HALF, ROW_TILE), lambda i: (0, i))],
        out_specs=[pl.BlockSpec((ROW_TILE, NAT_COLS), lambda i: (i, 0)),
                   pl.BlockSpec((QT_ROWS, ROW_TILE), lambda i: (0, i)),
                   pl.BlockSpec((VT_ROWS, ROW_TILE), lambda i: (0, i)),
                   pl.BlockSpec((nb, 1, 2 * LANES), lambda i: (i, 0, 0))],
        out_shape=[jax.ShapeDtypeStruct((s, NAT_COLS), bf16),
                   jax.ShapeDtypeStruct((QT_ROWS, s), bf16),
                   jax.ShapeDtypeStruct((VT_ROWS, s), bf16),
                   jax.ShapeDtypeStruct((s // MOBA_BLOCK, 1, 2 * LANES), f32)],
        compiler_params=_params(1),
        name="qkv_proj",
    )(x, gain, w_nat, w_t, hg_nat, hg_t, cos, sin, cos_t, sin_t)


def _split_rows(q_t):
    zero = jnp.zeros((HEAD_DIM, q_t.shape[1]), q_t.dtype)
    return (jnp.concatenate([q_t[:HEAD_DIM], zero], axis=0),
            jnp.concatenate([zero, q_t[HEAD_DIM:]], axis=0))


class _Slot:
    def __init__(self, s_ref, mx_ref):
        self.s, self.mx = s_ref, mx_ref


def _with_ones(v_t):
    return jnp.concatenate([v_t, jnp.ones((SUM_ROWS, v_t.shape[1]), v_t.dtype)], axis=0)


def _flash_stage(n_maps, state, score=None, nxt=None, cur=None, value=None):
    m_ref, acc_ref = state
    maps = range(n_maps)
    if cur is not None:
        m_old = [m_ref[h] for h in maps]
        m_new = [jnp.maximum(m_old[h], cur.mx[h]) for h in maps]
        for h in maps:
            acc_ref[h] = jnp.exp2(m_old[h] - m_new[h]) * acc_ref[h]
            m_ref[h] = m_new[h]
    col_max = [None] * n_maps

    def park(b):
        rows = slice(b * MOBA_BLOCK, (b + 1) * MOBA_BLOCK)
        for h in maps:
            s_b = score(b, h)
            nxt.s[h, rows, :] = s_b
            top = jnp.max(s_b, axis=0, keepdims=True)
            col_max[h] = top if b == 0 else jnp.maximum(col_max[h], top)
        if b == GROUP_BLOCKS - 1:
            for h in maps:
                nxt.mx[h] = col_max[h]

    def consume(b):
        rows = slice(b * MOBA_BLOCK, (b + 1) * MOBA_BLOCK)
        for h in maps:
            p = jnp.exp2(cur.s[h, rows, :] - m_new[h]).astype(bf16)
            acc_ref[h] += jnp.dot(_with_ones(value(b, h)), p, preferred_element_type=f32)

    if cur is not None and nxt is cur:
        for b in range(GROUP_BLOCKS):
            consume(b)
            park(b)
        return
    for step in range(GROUP_BLOCKS + PARK_LEAD):
        if score is not None and step < GROUP_BLOCKS:
            park(step)
        if cur is not None and step >= PARK_LEAD:
            consume(step - PARK_LEAD)


def _flash_pipeline(n_maps, first_tile, own_group, own_score, past_score, next_own_score, value, slots, state):
    m_ref, acc_ref = state
    m_ref[...] = jnp.full(m_ref.shape, NEG, f32)
    acc_ref[...] = jnp.zeros(acc_ref.shape, f32)

    def stage(base, offset, last):
        i = base + offset
        cur = slots[offset % 2]
        nxt = slots[0] if last else slots[1 - offset % 2]
        group = jnp.where(i == 0, own_group, i - 1)
        score = next_own_score if last else (lambda b, h: past_score(i, b, h))
        _flash_stage(n_maps, state, score=score, nxt=nxt, cur=cur, value=lambda b, h: value(group, b, h))

    pl.when(first_tile)(functools.partial(_flash_stage, n_maps, state, score=own_score, nxt=slots[0]))

    def trip(k, carry):
        for offset in range(STAGES_PER_TRIP):
            stage(k * STAGES_PER_TRIP, offset, False)
        return carry

    full_trips = own_group // STAGES_PER_TRIP
    lax.fori_loop(0, full_trips, trip, 0)

    base = full_trips * STAGES_PER_TRIP
    left = own_group - base
    for offset in range(STAGES_PER_TRIP):
        if offset > 0:
            pl.when(left >= offset)(functools.partial(stage, base, offset - 1, False))
        pl.when(left == offset)(functools.partial(stage, base, offset, True))


def _flash_scratch(n_maps, dv, tq):
    row = pltpu.VMEM((n_maps, 1, tq), f32)
    scores = pltpu.VMEM((n_maps, KEY_GROUP, tq), f32)
    return [row, pltpu.VMEM((n_maps, dv + SUM_ROWS, tq), f32), scores, row, scores, row]


def _positions(first_key, n_keys, t, tq):
    key = first_key + lax.broadcasted_iota(jnp.int32, (n_keys, tq), 0)
    query = t * tq + lax.broadcasted_iota(jnp.int32, (n_keys, tq), 1)
    return key, query


def _moba_kernel(qt_ref, qn_ref, k_ref, vt_ref, km_ref, o_ref, m_ref, acc_ref, sa_ref, mxa_ref, sb_ref, mxb_ref,
                 picks_ref):
    t = pl.program_id(1)
    t_next = jnp.minimum(t + 1, pl.num_programs(1) - 1)
    n_blocks = km_ref.shape[0]

    km = km_ref[...]
    km_hi = km.astype(bf16)
    rem = km - km_hi.astype(f32)
    km_mid = rem.astype(bf16)
    km_lo = (rem - km_mid.astype(f32)).astype(bf16)
    blk = lax.broadcasted_iota(jnp.int32, (n_blocks, 1), 0).astype(f32)

    def block_rows(g, b):
        return pl.ds(pl.multiple_of((g * GROUP_BLOCKS + b) * MOBA_BLOCK, MOBA_BLOCK), MOBA_BLOCK)

    def pick_blocks(tile, q_heads):
        tile_f = tile.astype(f32)
        picks = []
        for qh in q_heads:
            gate = (jnp.dot(km_hi, qh, preferred_element_type=f32)
                    + jnp.dot(km_mid, qh, preferred_element_type=f32)
                    + jnp.dot(km_lo, qh, preferred_element_type=f32))
            gate = jnp.where(blk < tile_f, gate, NEG)
            chosen = []
            for _ in range(MOBA_TOPK):
                best = jnp.max(gate, axis=0, keepdims=True)
                idx = jnp.min(jnp.where(gate == best, blk, 1e9), axis=0, keepdims=True)
                chosen.append(jnp.where(best > 0.5 * NEG, idx, -1.0))
                gate = jnp.where(blk == idx, 2.0 * NEG, gate)
            picks.append(chosen)
        return picks

    def tile_scores(tile, q_heads, picks):
        def past_score(g, b, h):
            s_b = jnp.dot(k_ref[block_rows(g, b), :], q_heads[h], preferred_element_type=f32)
            blk_f = (g * GROUP_BLOCKS + b).astype(f32)
            hit = (picks[h][0] == blk_f) | (picks[h][1] == blk_f) | (picks[h][2] == blk_f)
            return s_b, jnp.where(hit, s_b, NEG)

        def own_score(b, h):
            group = tile // GROUP_BLOCKS
            s_b, masked = past_score(group, b, h)
            key, query = _positions((group * GROUP_BLOCKS + b) * MOBA_BLOCK, MOBA_BLOCK, tile, ATT_TILE)
            return jnp.where(key >= tile * ATT_TILE, jnp.where(key <= query, s_b, NEG), masked)

        return own_score, lambda g, b, h: past_score(g, b, h)[1]

    def value(g, b, h):
        return vt_ref[h * HEAD_DIM:(h + 1) * HEAD_DIM, block_rows(g, b)]

    @pl.when(t == 0)
    def _():
        picks_ref[...] = jnp.full(picks_ref.shape, -1.0, f32)

    picks = [[picks_ref[h * MOBA_TOPK + i:h * MOBA_TOPK + i + 1, :] for i in range(MOBA_TOPK)] for h in range(2)]
    q_next = _split_rows(qn_ref[...])
    picks_next = pick_blocks(t_next, q_next)
    own_score, past_score = tile_scores(t, _split_rows(qt_ref[...]), picks)
    next_own_score, _ = tile_scores(t_next, q_next, picks_next)
    _flash_pipeline(2, t == 0, t // GROUP_BLOCKS, own_score, past_score, next_own_score, value,
                    (_Slot(sa_ref, mxa_ref), _Slot(sb_ref, mxb_ref)), (m_ref, acc_ref))
    for h in range(2):
        for i in range(MOBA_TOPK):
            picks_ref[h * MOBA_TOPK + i:h * MOBA_TOPK + i + 1, :] = picks_next[h][i]

    heads = [acc_ref[h, :HEAD_DIM, :] / acc_ref[h, HEAD_DIM:HEAD_DIM + 1, :] for h in range(2)]
    o_ref[...] = jnp.concatenate(heads, axis=0).T.astype(bf16)


def _moba(nat, q_t, v_t, kmean):
    s = nat.shape[0]
    n_blocks = s // MOBA_BLOCK
    last = s // ATT_TILE - 1
    return pl.pallas_call(
        _moba_kernel,
        grid=(2, s // ATT_TILE),
        in_specs=[pl.BlockSpec((LANES, ATT_TILE), lambda p, t: (p, t)),
                  pl.BlockSpec((LANES, ATT_TILE), lambda p, t: (p, jnp.minimum(t + 1, last))),
                  pl.BlockSpec((s, LANES), lambda p, t: (0, NAT_KA + p)),
                  pl.BlockSpec((LANES, s), lambda p, t: (p, 0)),
                  pl.BlockSpec((n_blocks, LANES), lambda p, t: (0, p))],
        out_specs=pl.BlockSpec((ATT_TILE, LANES), lambda p, t: (t, p)),
        out_shape=jax.ShapeDtypeStruct((s, 2 * LANES), bf16),
        scratch_shapes=_flash_scratch(2, HEAD_DIM, ATT_TILE) + [pltpu.VMEM((2 * MOBA_TOPK, ATT_TILE), f32)],
        compiler_params=_params(2),
        name="moba_attention",
    )(q_t, q_t, nat, v_t, kmean)


def _swa_kernel(sink_ref, q_ref, k_ref, kp_ref, v_ref, vp_ref, o_ref):
    p_id = pl.program_id(0)
    t = pl.program_id(1)
    sub_tiles = SWA_TILE // SWA_WINDOW
    lane = lax.broadcasted_iota(jnp.int32, (1, LANES), 1)
    zero = jnp.zeros((SWA_WINDOW, LANES), bf16)
    qi = lax.broadcasted_iota(jnp.int32, (SWA_WINDOW, 2 * SWA_WINDOW), 0)
    kj = lax.broadcasted_iota(jnp.int32, (SWA_WINDOW, 2 * SWA_WINDOW), 1)
    back = qi + SWA_WINDOW - kj
    in_window = (back >= 0) & (back < SWA_WINDOW)
    for sub in range(sub_tiles):
        rows = slice(sub * SWA_WINDOW, (sub + 1) * SWA_WINDOW)
        if sub == 0:
            k_prev, v_prev = kp_ref[...], vp_ref[...]
            mask = in_window & (kj >= SWA_WINDOW - t * SWA_TILE)
        else:
            prev = slice((sub - 1) * SWA_WINDOW, sub * SWA_WINDOW)
            k_prev, v_prev = k_ref[prev, :], v_ref[prev, :]
            mask = in_window
        k_band = jnp.concatenate([k_prev, k_ref[rows, :]], axis=0)
        v_band = jnp.concatenate([v_prev, v_ref[rows, :]], axis=0)
        q = q_ref[rows, :]
        outs = []
        for h, qh in enumerate((jnp.where(lane < HEAD_DIM, q, zero), jnp.where(lane < HEAD_DIM, zero, q))):
            sink = sink_ref[2 * p_id + h]
            s = lax.dot_general(qh, k_band, _NT, preferred_element_type=f32)
            s = jnp.where(mask, s, NEG)
            m = jnp.maximum(jnp.max(s, axis=-1, keepdims=True), sink)
            p = jnp.exp(s - m)
            denom = jnp.sum(p, axis=-1, keepdims=True) + jnp.exp(sink - m)
            outs.append(jnp.dot(p.astype(bf16), v_band, preferred_element_type=f32) / denom)
        o_ref[rows, :] = jnp.where(lane < HEAD_DIM, outs[0], outs[1]).astype(bf16)


def _swa(nat, sinks):
    s = nat.shape[0]
    ratio = SWA_TILE // SWA_WINDOW

    def prev_rows(t):
        return jnp.maximum(t * ratio - 1, 0)

    return pl.pallas_call(
        _swa_kernel,
        grid=(2, s // SWA_TILE),
        in_specs=[pl.BlockSpec(memory_space=pltpu.SMEM),
                  pl.BlockSpec((SWA_TILE, LANES), lambda p, t: (t, NAT_QB + p)),
                  pl.BlockSpec((SWA_TILE, LANES), lambda p, t: (t, NAT_KB + p)),
                  pl.BlockSpec((SWA_WINDOW, LANES), lambda p, t: (prev_rows(t), NAT_KB + p)),
                  pl.BlockSpec((SWA_TILE, LANES), lambda p, t: (t, NAT_VB + p)),
                  pl.BlockSpec((SWA_WINDOW, LANES), lambda p, t: (prev_rows(t), NAT_VB + p))],
        out_specs=pl.BlockSpec((SWA_TILE, LANES), lambda p, t: (t, p)),
        out_shape=jax.ShapeDtypeStruct((s, 2 * LANES), bf16),
        compiler_params=_params(2),
        name="swa_attention",
    )(sinks, nat, nat, nat, nat, nat)


def _diff_kernel(lam_init, qt_ref, qn_ref, k_ref, vt_ref, lq1_ref, lk1_ref, lq2_ref, lk2_ref, sub_ref,
                 o_ref, m_ref, acc_ref, sa_ref, mxa_ref, sb_ref, mxb_ref):
    t = pl.program_id(1)
    t_next = jnp.minimum(t + 1, pl.num_programs(1) - 1)

    def block_rows(g, b):
        return pl.ds(pl.multiple_of((g * GROUP_BLOCKS + b) * MOBA_BLOCK, MOBA_BLOCK), MOBA_BLOCK)

    def tile_scores(tile, q_ref):
        q_maps = _split_rows(q_ref[...])

        def past_score(g, b, h):
            return jnp.dot(k_ref[block_rows(g, b), :], q_maps[h], preferred_element_type=f32)

        def own_score(b, h):
            group = tile * DIFF_TILE // KEY_GROUP
            key, query = _positions((group * GROUP_BLOCKS + b) * MOBA_BLOCK, MOBA_BLOCK, tile, DIFF_TILE)
            return jnp.where(key <= query, past_score(group, b, h), NEG)

        return own_score, past_score

    def value(g, b, h):
        return vt_ref[:, block_rows(g, b)]

    own_score, past_score = tile_scores(t, qt_ref)
    next_own_score, _ = tile_scores(t_next, qn_ref)
    _flash_pipeline(2, t == 0, t * DIFF_TILE // KEY_GROUP, own_score, past_score, next_own_score, value,
                    (_Slot(sa_ref, mxa_ref), _Slot(sb_ref, mxb_ref)), (m_ref, acc_ref))

    lam = (jnp.exp(jnp.sum(lq1_ref[...] * lk1_ref[...], axis=-1, keepdims=True))
           - jnp.exp(jnp.sum(lq2_ref[...] * lk2_ref[...], axis=-1, keepdims=True)) + lam_init)
    maps = [acc_ref[h, :LANES, :] / acc_ref[h, LANES:LANES + 1, :] for h in range(2)]
    out_t = maps[0] - lam * maps[1]
    out = _rms(out_t.T, sub_ref[...]) * (1.0 - lam_init)
    o_ref[...] = out.astype(bf16)


def _diff(nat, q_t, v_t, lq1, lk1, lq2, lk2, subln, lam_init):
    s = nat.shape[0]
    vec = _const_spec((1, HEAD_DIM))
    last = s // DIFF_TILE - 1
    return pl.pallas_call(
        functools.partial(_diff_kernel, lam_init),
        grid=(4, s // DIFF_TILE),
        in_specs=[pl.BlockSpec((LANES, DIFF_TILE), lambda h, t: (2 + h, t)),
                  pl.BlockSpec((LANES, DIFF_TILE), lambda h, t: (2 + h, jnp.minimum(t + 1, last))),
                  pl.BlockSpec((s, LANES), lambda h, t: (0, NAT_KC + h)),
                  pl.BlockSpec((LANES, s), lambda h, t: (2 + h, 0)),
                  vec, vec, vec, vec, _const_spec((1, LANES))],
        out_specs=pl.BlockSpec((DIFF_TILE, LANES), lambda h, t: (t, h)),
        out_shape=jax.ShapeDtypeStruct((s, 4 * LANES), bf16),
        scratch_shapes=_flash_scratch(2, LANES, DIFF_TILE),
        compiler_params=_params(2),
        name="diff_attention",
    )(q_t, q_t, nat, v_t, lq1, lk1, lq2, lk2, subln)


def _merge_kernel(x_ref, g_ref, wgate_ref, oa_ref, ob_ref, oc_ref, pa_ref, pb_ref, pc_ref, wo_ref, o_ref):
    x = x_ref[...]
    h = _rms(x, g_ref[...]).astype(bf16)
    branches = ((oa_ref, pa_ref), (ob_ref, pb_ref), (oc_ref, pc_ref))
    merged = jnp.zeros_like(x)
    for i, (b_ref, p_ref) in enumerate(branches):
        logits = jnp.dot(h, wgate_ref[:, i * D_MODEL:(i + 1) * D_MODEL], preferred_element_type=f32)
        y = jnp.dot(b_ref[...], p_ref[...], preferred_element_type=f32)
        merged = merged + jax.nn.sigmoid(logits) * y
    o_ref[...] = x + jnp.dot(merged.astype(bf16), wo_ref[...], preferred_element_type=f32)


def _merge(x, gain, w_gate, oa, ob, oc, pa, pb, pc, w_out):
    s = x.shape[0]

    def rows(width):
        return pl.BlockSpec((ROW_TILE, width), lambda i: (i, 0))

    return pl.pallas_call(
        _merge_kernel,
        grid=(s // ROW_TILE,),
        in_specs=[rows(D_MODEL), _const_spec((1, D_MODEL)), _const_spec((D_MODEL, GATE_COLS)),
                  rows(oa.shape[1]), rows(ob.shape[1]), rows(oc.shape[1]),
                  _const_spec(pa.shape), _const_spec(pb.shape), _const_spec(pc.shape),
                  _const_spec((D_MODEL, D_MODEL))],
        out_specs=rows(D_MODEL),
        out_shape=jax.ShapeDtypeStruct(x.shape, f32),
        compiler_params=_params(1),
        name="gated_merge",
    )(x, gain, w_gate, oa, ob, oc, pa, pb, pc, w_out)


def _rope_tables(seq):
    pos = jnp.arange(seq, dtype=f32)
    inv_freq = ROPE_THETA ** (-jnp.arange(0, HEAD_DIM, 2, dtype=f32) / HEAD_DIM)
    ang = pos[:, None] * inv_freq[None, :]
    cos, sin = jnp.cos(ang), jnp.sin(ang)
    return (jnp.tile(cos, (1, 4)), jnp.concatenate([-sin, sin, -sin, sin], axis=-1), cos.T, sin.T)


def _layer_weights(w_in):
    qa, ka, va = w_in[:, 0:256], w_in[:, 256:512], w_in[:, 512:768]
    qb, kb, vb = w_in[:, 768:1024], w_in[:, 1024:1152], w_in[:, 1152:1280]
    qc, kc, vc = w_in[:, 1280:1792], w_in[:, 1792:2304], w_in[:, 2304:2816]
    gates = w_in[:, 2816:]

    def dup(w):
        h0, h1 = w[:, :HEAD_DIM], w[:, HEAD_DIM:]
        return jnp.concatenate([h0, h0, h1, h1], axis=1)

    w_nat = jnp.concatenate([ka, kc, qb, dup(kb), dup(vb)], axis=1)
    w_t = jnp.concatenate([qa, qc, va, vc], axis=1).T
    return w_nat.astype(bf16), w_t.astype(bf16), gates.astype(bf16)


def _head_gains(qa, ka, qb, kb, qc, kc):
    scale = HEAD_DIM ** -0.5
    scale2 = scale * math.log2(math.e)
    nat = jnp.concatenate([jnp.tile(ka, 4), jnp.tile(kc, 8), jnp.tile(qb * scale, 4), jnp.tile(kb, 4)])
    feat = jnp.concatenate([jnp.tile(qa * scale2, 4), jnp.tile(qc * scale2, 8)])
    return nat[None, :], feat[:, None]


def kernel(x, ffn1_norm, ffn1_w_gate, ffn1_w_up, ffn1_w_down, mix_norm, w_in, moba_q_norm, moba_k_norm, swa_q_norm, swa_k_norm, swa_sinks, diff_q_norm, diff_k_norm, diff_lambda_q1, diff_lambda_k1, diff_lambda_q2, diff_lambda_k2, diff_subln, w_branch_a, w_branch_b, w_branch_c, w_out, ffn2_norm, ffn2_w_gate, ffn2_w_up, ffn2_w_down):
    batch, seq, _ = x.shape
    assert batch == 1 and seq % SWA_TILE == 0 and seq % ROW_TILE == 0 and seq % KEY_GROUP == 0
    depth = w_in.shape[0]
    cos, sin, cos_t, sin_t = _rope_tables(seq)
    xs = x[0]
    for l in range(depth):
        lam_init = 0.8 - 0.6 * math.exp(-0.3 * l)
        xs = _ffn(xs, ffn1_norm[l][None], ffn1_w_gate[l].astype(bf16), ffn1_w_up[l].astype(bf16),
                  ffn1_w_down[l].astype(bf16))
        w_nat, w_t, w_gate = _layer_weights(w_in[l])
        hg_nat, hg_t = _head_gains(moba_q_norm[l], moba_k_norm[l], swa_q_norm[l], swa_k_norm[l],
                                   diff_q_norm[l], diff_k_norm[l])
        nat, q_t, v_t, kmean = _proj(xs, mix_norm[l][None], w_nat, w_t, hg_nat, hg_t, cos, sin, cos_t, sin_t)
        oa = _moba(nat, q_t, v_t, kmean.reshape(seq // MOBA_BLOCK, 2 * LANES))
        ob = _swa(nat, swa_sinks[l])
        oc = _diff(nat, q_t, v_t, diff_lambda_q1[l][None], diff_lambda_k1[l][None], diff_lambda_q2[l][None],
                   diff_lambda_k2[l][None], diff_subln[l][None], lam_init)
        xs = _merge(xs, mix_norm[l][None], w_gate, oa, ob, oc, w_branch_a[l].astype(bf16),
                    w_branch_b[l].astype(bf16), w_branch_c[l].astype(bf16), w_out[l].astype(bf16))
        xs = _ffn(xs, ffn2_norm[l][None], ffn2_w_gate[l].astype(bf16), ffn2_w_up[l].astype(bf16),
                  ffn2_w_down[l].astype(bf16))
    return xs[None]
```

```python
import functools
import math

import jax
import jax.numpy as jnp
from jax import lax
from jax.experimental import pallas as pl
from jax.experimental.pallas import tpu as pltpu

D_MODEL = 1024
D_FF = 2816
HEAD_DIM = 64
HALF = HEAD_DIM // 2
LANES = 128
MOBA_BLOCK = 256
MOBA_TOPK = 3
SWA_WINDOW = 128
N_BRANCH = 3
ROPE_THETA = 10000.0
EPS = 1e-6
NEG = -1e30
FFN_HALF = 0.5

NAT_QK_COLS = 1280
NAT_COLS = 1536
NAT_KA, NAT_KC, NAT_QB, NAT_KB, NAT_VB = 0, 2, 6, 8, 10
QT_ROWS = 768
VT_ROWS = 768
GATE_COLS = N_BRANCH * D_MODEL

ROW_TILE = 512
MOBA_TILE = 512
DIFF_TILE = 512
GROUP_BLOCKS = 4
KEY_GROUP = GROUP_BLOCKS * MOBA_BLOCK
STAGES_PER_TRIP = 4
PARK_LEAD = 1
SUM_ROWS = 16
SWA_TILE = 512
VMEM_LIMIT = 48 * 1024 * 1024

f32 = jnp.float32
bf16 = jnp.bfloat16

_NT = (((1,), (1,)), ((), ()))


def _rms(x, gain):
    return x * lax.rsqrt(jnp.mean(x * x, axis=-1, keepdims=True) + EPS) * gain


def _params(n_axes):
    return pltpu.CompilerParams(
        dimension_semantics=("arbitrary",) * n_axes, vmem_limit_bytes=VMEM_LIMIT)


def _const_spec(shape):
    return pl.BlockSpec(shape, lambda *_: (0,) * len(shape), pipeline_mode=pl.Buffered(1))


def _ffn_kernel(x_ref, g_ref, wg_ref, wu_ref, wd_ref, o_ref):
    x = x_ref[...]
    h = _rms(x, g_ref[...]).astype(bf16)
    a = jnp.dot(h, wg_ref[...], preferred_element_type=f32)
    b = jnp.dot(h, wu_ref[...], preferred_element_type=f32)
    act = (a * jax.nn.sigmoid(a) * b).astype(bf16)
    o_ref[...] = x + FFN_HALF * jnp.dot(act, wd_ref[...], preferred_element_type=f32)


def _layer_spec(shape, layer):
    return pl.BlockSpec((None,) + tuple(shape), lambda *_: (layer,) + (0,) * len(shape),
                        pipeline_mode=pl.Buffered(1))


def _ffn(x, gain, wg, wu, wd, layer):
    s = x.shape[0]
    row = pl.BlockSpec((ROW_TILE, D_MODEL), lambda i: (i, 0))
    return pl.pallas_call(
        _ffn_kernel,
        grid=(s // ROW_TILE,),
        in_specs=[row, _const_spec((1, D_MODEL)), _layer_spec((D_MODEL, D_FF), layer),
                  _layer_spec((D_MODEL, D_FF), layer), _layer_spec((D_FF, D_MODEL), layer)],
        out_specs=row,
        out_shape=jax.ShapeDtypeStruct(x.shape, f32),
        compiler_params=_params(1),
        name="ffn",
    )(x, gain, wg, wu, wd)


def _proj_kernel(x_ref, g_ref, wn_ref, wt_ref, hg_ref, hgt_ref, cos_ref, sin_ref, cost_ref, sint_ref,
                 nat_ref, qt_ref, vt_ref, km_ref):
    x = x_ref[...]
    h = _rms(x, g_ref[...]).astype(bf16)

    proj = jnp.dot(h, wn_ref[...], preferred_element_type=f32)
    lane = lax.broadcasted_iota(jnp.int32, (1, LANES), 1)
    first_head = lane < HEAD_DIM
    first_half = (lane % HEAD_DIM) < HALF
    cos = cos_ref[...]
    sin = sin_ref[...]
    for c in range(NAT_QK_COLS // LANES):
        cols = slice(c * LANES, (c + 1) * LANES)
        y = proj[:, cols]
        sq = y * y
        ss0 = jnp.sum(jnp.where(first_head, sq, 0.0), axis=-1, keepdims=True)
        ss1 = jnp.sum(jnp.where(first_head, 0.0, sq), axis=-1, keepdims=True)
        inv = jnp.where(first_head, lax.rsqrt(ss0 / HEAD_DIM + EPS), lax.rsqrt(ss1 / HEAD_DIM + EPS))
        y = y * inv * hg_ref[:, cols]
        partner = jnp.where(first_half, pltpu.roll(y, LANES - HALF, 1), pltpu.roll(y, HALF, 1))
        y = y * cos + partner * sin
        nat_ref[:, cols] = y.astype(bf16)
        if c in (NAT_KA, NAT_KA + 1):
            for b in range(ROW_TILE // MOBA_BLOCK):
                blk = y[b * MOBA_BLOCK:(b + 1) * MOBA_BLOCK]
                km_ref[b, :, (c - NAT_KA) * LANES:(c - NAT_KA + 1) * LANES] = jnp.mean(blk, axis=0, keepdims=True)
    nat_ref[:, NAT_QK_COLS:] = proj[:, NAT_QK_COLS:].astype(bf16)

    proj_t = lax.dot_general(wt_ref[...], h, _NT, preferred_element_type=f32)
    cos_t = cost_ref[...]
    sin_t = sint_ref[...]
    for c in range(QT_ROWS // HEAD_DIM):
        rows = slice(c * HEAD_DIM, (c + 1) * HEAD_DIM)
        y = proj_t[rows, :]
        inv = lax.rsqrt(jnp.mean(y * y, axis=0, keepdims=True) + EPS)
        y = y * inv * hgt_ref[rows, :]
        y1, y2 = y[:HALF], y[HALF:]
        qt_ref[c * HEAD_DIM:c * HEAD_DIM + HALF, :] = (y1 * cos_t - y2 * sin_t).astype(bf16)
        qt_ref[c * HEAD_DIM + HALF:(c + 1) * HEAD_DIM, :] = (y2 * cos_t + y1 * sin_t).astype(bf16)
    vt_ref[...] = proj_t[QT_ROWS:, :].astype(bf16)


def _proj(x, gain, w_nat, w_t, hg_nat, hg_t, cos, sin, cos_t, sin_t):
    s = x.shape[0]
    nb = ROW_TILE // MOBA_BLOCK
    return pl.pallas_call(
        _proj_kernel,
        grid=(s // ROW_TILE,),
        in_specs=[pl.BlockSpec((ROW_TILE, D_MODEL), lambda i: (i, 0)),
                  _const_spec((1, D_MODEL)),
                  _const_spec((D_MODEL, NAT_COLS)),
                  _const_spec((QT_ROWS + VT_ROWS, D_MODEL)),
                  _const_spec((1, NAT_QK_COLS)),
                  _const_spec((QT_ROWS, 1)),
                  pl.BlockSpec((ROW_TILE, LANES), lambda i: (i, 0)),
                  pl.BlockSpec((ROW_TILE, LANES), lambda i: (i, 0)),
                  pl.BlockSpec((HALF, ROW_TILE), lambda i: (0, i)),
                  pl.BlockSpec((HALF, ROW_TILE), lambda i: (0, i))],
        out_specs=[pl.BlockSpec((ROW_TILE, NAT_COLS), lambda i: (i, 0)),
                   pl.BlockSpec((QT_ROWS, ROW_TILE), lambda i: (0, i)),
                   pl.BlockSpec((VT_ROWS, ROW_TILE), lambda i: (0, i)),
                   pl.BlockSpec((nb, 1, 2 * LANES), lambda i: (i, 0, 0))],
        out_shape=[jax.ShapeDtypeStruct((s, NAT_COLS), bf16),
                   jax.ShapeDtypeStruct((QT_ROWS, s), bf16),
                   jax.ShapeDtypeStruct((VT_ROWS, s), bf16),
                   jax.ShapeDtypeStruct((s // MOBA_BLOCK, 1, 2 * LANES), f32)],
        compiler_params=_params(1),
        name="qkv_proj",
    )(x, gain, w_nat, w_t, hg_nat, hg_t, cos, sin, cos_t, sin_t)


def _split_rows(q_t):
    zero = jnp.zeros((HEAD_DIM, q_t.shape[1]), q_t.dtype)
    return (jnp.concatenate([q_t[:HEAD_DIM], zero], axis=0),
            jnp.concatenate([zero, q_t[HEAD_DIM:]], axis=0))


class _Slot:
    def __init__(self, s_ref, mx_ref):
        self.s, self.mx = s_ref, mx_ref


def _with_ones(v_t):
    return jnp.concatenate([v_t, jnp.ones((SUM_ROWS, v_t.shape[1]), v_t.dtype)], axis=0)


def _flash_stage(n_maps, state, score=None, nxt=None, cur=None, value=None):
    m_ref, acc_ref = state
    maps = range(n_maps)
    if cur is not None:
        m_old = [m_ref[h] for h in maps]
        m_new = [jnp.maximum(m_old[h], cur.mx[h]) for h in maps]
        for h in maps:
            acc_ref[h] = jnp.exp2(m_old[h] - m_new[h]) * acc_ref[h]
            m_ref[h] = m_new[h]
    col_max = [None] * n_maps

    def park(b):
        rows = slice(b * MOBA_BLOCK, (b + 1) * MOBA_BLOCK)
        for h in maps:
            s_b = score(b, h)
            nxt.s[h, rows, :] = s_b
            top = jnp.max(s_b, axis=0, keepdims=True)
            col_max[h] = top if b == 0 else jnp.maximum(col_max[h], top)
        if b == GROUP_BLOCKS - 1:
            for h in maps:
                nxt.mx[h] = col_max[h]

    def consume(b):
        rows = slice(b * MOBA_BLOCK, (b + 1) * MOBA_BLOCK)
        for h in maps:
            p = jnp.exp2(cur.s[h, rows, :] - m_new[h]).astype(bf16)
            acc_ref[h] += jnp.dot(_with_ones(value(b, h)), p, preferred_element_type=f32)

    if cur is not None and nxt is cur:
        for b in range(GROUP_BLOCKS):
            consume(b)
            park(b)
        return
    for step in range(GROUP_BLOCKS + PARK_LEAD):
        if score is not None and step < GROUP_BLOCKS:
            park(step)
        if cur is not None and step >= PARK_LEAD:
            consume(step - PARK_LEAD)


def _flash_pipeline(n_maps, first_tile, own_group, own_score, past_score, next_own_score, value, slots, state):
    m_ref, acc_ref = state
    m_ref[...] = jnp.full(m_ref.shape, NEG, f32)
    acc_ref[...] = jnp.zeros(acc_ref.shape, f32)

    def stage(base, offset, last):
        i = base + offset
        cur = slots[offset % 2]
        nxt = slots[0] if last else slots[1 - offset % 2]
        group = jnp.where(i == 0, own_group, i - 1)
        score = next_own_score if last else (lambda b, h: past_score(i, b, h))
        _flash_stage(n_maps, state, score=score, nxt=nxt, cur=cur, value=lambda b, h: value(group, b, h))

    pl.when(first_tile)(functools.partial(_flash_stage, n_maps, state, score=own_score, nxt=slots[0]))

    def trip(k, carry):
        for offset in range(STAGES_PER_TRIP):
            stage(k * STAGES_PER_TRIP, offset, False)
        return carry

    full_trips = own_group // STAGES_PER_TRIP
    lax.fori_loop(0, full_trips, trip, 0)

    base = full_trips * STAGES_PER_TRIP
    left = own_group - base
    for offset in range(STAGES_PER_TRIP):
        if offset > 0:
            pl.when(left >= offset)(functools.partial(stage, base, offset - 1, False))
        pl.when(left == offset)(functools.partial(stage, base, offset, True))


def _flash_scratch(n_maps, dv, tq):
    row = pltpu.VMEM((n_maps, 1, tq), f32)
    scores = pltpu.VMEM((n_maps, KEY_GROUP, tq), f32)
    return [row, pltpu.VMEM((n_maps, dv + SUM_ROWS, tq), f32), scores, row, scores, row]


def _positions(first_key, n_keys, t, tq):
    key = first_key + lax.broadcasted_iota(jnp.int32, (n_keys, tq), 0)
    query = t * tq + lax.broadcasted_iota(jnp.int32, (n_keys, tq), 1)
    return key, query


def _moba_kernel(qt_ref, qn_ref, k_ref, vt_ref, km_ref, o_ref, m_ref, acc_ref, sa_ref, mxa_ref, sb_ref, mxb_ref,
                 picks_ref):
    t = pl.program_id(1)
    t_next = jnp.minimum(t + 1, pl.num_programs(1) - 1)
    n_blocks = km_ref.shape[0]

    km = km_ref[...]
    km_hi = km.astype(bf16)
    rem = km - km_hi.astype(f32)
    km_mid = rem.astype(bf16)
    km_lo = (rem - km_mid.astype(f32)).astype(bf16)
    blk = lax.broadcasted_iota(jnp.int32, (n_blocks, 1), 0).astype(f32)

    def block_rows(g, b):
        return pl.ds(pl.multiple_of((g * GROUP_BLOCKS + b) * MOBA_BLOCK, MOBA_BLOCK), MOBA_BLOCK)

    def query_row(tile):
        return tile * MOBA_TILE + lax.broadcasted_iota(jnp.int32, (1, MOBA_TILE), 1)

    def pick_blocks(tile, q_heads):
        own_blk = (query_row(tile) // MOBA_BLOCK).astype(f32)
        picks = []
        for qh in q_heads:
            gate = (jnp.dot(km_hi, qh, preferred_element_type=f32)
                    + jnp.dot(km_mid, qh, preferred_element_type=f32)
                    + jnp.dot(km_lo, qh, preferred_element_type=f32))
            gate = jnp.where(blk < own_blk, gate, NEG)
            chosen = []
            for _ in range(MOBA_TOPK):
                best = jnp.max(gate, axis=0, keepdims=True)
                idx = jnp.min(jnp.where(gate == best, blk, 1e9), axis=0, keepdims=True)
                chosen.append(jnp.where(best > 0.5 * NEG, idx, -1.0))
                gate = jnp.where(blk == idx, 2.0 * NEG, gate)
            picks.append(chosen)
        return picks

    def tile_scores(tile, q_heads, picks):
        def picked(g, b, h):
            blk_f = (g * GROUP_BLOCKS + b).astype(f32)
            return (picks[h][0] == blk_f) | (picks[h][1] == blk_f) | (picks[h][2] == blk_f)

        def past_score(g, b, h):
            bias = jnp.where(picked(g, b, h), 0.0, NEG)
            first_row = lax.broadcasted_iota(jnp.int32, (SUM_ROWS, MOBA_TILE), 0) == 0
            bias_rows = jnp.where(first_row, bias, 0.0).astype(bf16)
            rest = jnp.zeros((LANES - SUM_ROWS, MOBA_TILE), bf16)
            rhs = jnp.concatenate([q_heads[h], bias_rows, rest], axis=0)
            lhs = jnp.concatenate([k_ref[block_rows(g, b), :], jnp.ones((MOBA_BLOCK, LANES), bf16)], axis=1)
            return jnp.dot(lhs, rhs, preferred_element_type=f32)

        def own_score(b, h):
            group = tile * MOBA_TILE // KEY_GROUP
            s_b = jnp.dot(k_ref[block_rows(group, b), :], q_heads[h], preferred_element_type=f32)
            key, query = _positions((group * GROUP_BLOCKS + b) * MOBA_BLOCK, MOBA_BLOCK, tile, MOBA_TILE)
            own_start = query_row(tile) // MOBA_BLOCK * MOBA_BLOCK
            masked = jnp.where(picked(group, b, h), s_b, NEG)
            return jnp.where(key >= own_start, jnp.where(key <= query, s_b, NEG), masked)

        return own_score, past_score

    def value(g, b, h):
        return vt_ref[h * HEAD_DIM:(h + 1) * HEAD_DIM, block_rows(g, b)]

    def store_picks(picks):
        for h in range(2):
            for i in range(MOBA_TOPK):
                picks_ref[h * MOBA_TOPK + i:h * MOBA_TOPK + i + 1, :] = picks[h][i]

    q_heads = _split_rows(qt_ref[...])
    pl.when(t == 0)(lambda: store_picks(pick_blocks(t, q_heads)))
    picks = [[picks_ref[h * MOBA_TOPK + i:h * MOBA_TOPK + i + 1, :] for i in range(MOBA_TOPK)] for h in range(2)]
    q_next = _split_rows(qn_ref[...])
    picks_next = pick_blocks(t_next, q_next)
    own_score, past_score = tile_scores(t, q_heads, picks)
    next_own_score, _ = tile_scores(t_next, q_next, picks_next)
    _flash_pipeline(2, t == 0, t * MOBA_TILE // KEY_GROUP, own_score, past_score, next_own_score, value,
                    (_Slot(sa_ref, mxa_ref), _Slot(sb_ref, mxb_ref)), (m_ref, acc_ref))
    store_picks(picks_next)

    heads = [acc_ref[h, :HEAD_DIM, :] / acc_ref[h, HEAD_DIM:HEAD_DIM + 1, :] for h in range(2)]
    o_ref[...] = jnp.concatenate(heads, axis=0).T.astype(bf16)


def _moba(nat, q_t, v_t, kmean):
    s = nat.shape[0]
    n_blocks = s // MOBA_BLOCK
    last = s // MOBA_TILE - 1
    return pl.pallas_call(
        _moba_kernel,
        grid=(2, s // MOBA_TILE),
        in_specs=[pl.BlockSpec((LANES, MOBA_TILE), lambda p, t: (p, t)),
                  pl.BlockSpec((LANES, MOBA_TILE), lambda p, t: (p, jnp.minimum(t + 1, last))),
                  pl.BlockSpec((s, LANES), lambda p, t: (0, NAT_KA + p)),
                  pl.BlockSpec((LANES, s), lambda p, t: (p, 0)),
                  pl.BlockSpec((n_blocks, LANES), lambda p, t: (0, p))],
        out_specs=pl.BlockSpec((MOBA_TILE, LANES), lambda p, t: (t, p)),
        out_shape=jax.ShapeDtypeStruct((s, 2 * LANES), bf16),
        scratch_shapes=_flash_scratch(2, HEAD_DIM, MOBA_TILE) + [pltpu.VMEM((2 * MOBA_TOPK, MOBA_TILE), f32)],
        compiler_params=_params(2),
        name="moba_attention",
    )(q_t, q_t, nat, v_t, kmean)


def _swa_kernel(sink_ref, q_ref, k_ref, kp_ref, v_ref, vp_ref, o_ref):
    p_id = pl.program_id(0)
    t = pl.program_id(1)
    sub_tiles = SWA_TILE // SWA_WINDOW
    lane = lax.broadcasted_iota(jnp.int32, (1, LANES), 1)
    zero = jnp.zeros((SWA_WINDOW, LANES), bf16)
    qi = lax.broadcasted_iota(jnp.int32, (SWA_WINDOW, 2 * SWA_WINDOW), 0)
    kj = lax.broadcasted_iota(jnp.int32, (SWA_WINDOW, 2 * SWA_WINDOW), 1)
    back = qi + SWA_WINDOW - kj
    in_window = (back >= 0) & (back < SWA_WINDOW)
    for sub in range(sub_tiles):
        rows = slice(sub * SWA_WINDOW, (sub + 1) * SWA_WINDOW)
        if sub == 0:
            k_prev, v_prev = kp_ref[...], vp_ref[...]
            mask = in_window & (kj >= SWA_WINDOW - t * SWA_TILE)
        else:
            prev = slice((sub - 1) * SWA_WINDOW, sub * SWA_WINDOW)
            k_prev, v_prev = k_ref[prev, :], v_ref[prev, :]
            mask = in_window
        k_band = jnp.concatenate([k_prev, k_ref[rows, :]], axis=0)
        v_band = jnp.concatenate([v_prev, v_ref[rows, :]], axis=0)
        q = q_ref[rows, :]
        outs = []
        for h, qh in enumerate((jnp.where(lane < HEAD_DIM, q, zero), jnp.where(lane < HEAD_DIM, zero, q))):
            sink = sink_ref[2 * p_id + h]
            s = lax.dot_general(qh, k_band, _NT, preferred_element_type=f32)
            s = jnp.where(mask, s, NEG)
            m = jnp.maximum(jnp.max(s, axis=-1, keepdims=True), sink)
            p = jnp.exp(s - m)
            denom = jnp.sum(p, axis=-1, keepdims=True) + jnp.exp(sink - m)
            outs.append(jnp.dot(p.astype(bf16), v_band, preferred_element_type=f32) / denom)
        o_ref[rows, :] = jnp.where(lane < HEAD_DIM, outs[0], outs[1]).astype(bf16)


def _swa(nat, sinks):
    s = nat.shape[0]
    ratio = SWA_TILE // SWA_WINDOW

    def prev_rows(t):
        return jnp.maximum(t * ratio - 1, 0)

    return pl.pallas_call(
        _swa_kernel,
        grid=(2, s // SWA_TILE),
        in_specs=[pl.BlockSpec(memory_space=pltpu.SMEM),
                  pl.BlockSpec((SWA_TILE, LANES), lambda p, t: (t, NAT_QB + p)),
                  pl.BlockSpec((SWA_TILE, LANES), lambda p, t: (t, NAT_KB + p)),
                  pl.BlockSpec((SWA_WINDOW, LANES), lambda p, t: (prev_rows(t), NAT_KB + p)),
                  pl.BlockSpec((SWA_TILE, LANES), lambda p, t: (t, NAT_VB + p)),
                  pl.BlockSpec((SWA_WINDOW, LANES), lambda p, t: (prev_rows(t), NAT_VB + p))],
        out_specs=pl.BlockSpec((SWA_TILE, LANES), lambda p, t: (t, p)),
        out_shape=jax.ShapeDtypeStruct((s, 2 * LANES), bf16),
        compiler_params=_params(2),
        name="swa_attention",
    )(sinks, nat, nat, nat, nat, nat)


def _diff_kernel(lam_init, qt_ref, qn_ref, k_ref, vt_ref, lq1_ref, lk1_ref, lq2_ref, lk2_ref, sub_ref,
                 o_ref, m_ref, acc_ref, sa_ref, mxa_ref, sb_ref, mxb_ref):
    t = pl.program_id(1)
    t_next = jnp.minimum(t + 1, pl.num_programs(1) - 1)

    def block_rows(g, b):
        return pl.ds(pl.multiple_of((g * GROUP_BLOCKS + b) * MOBA_BLOCK, MOBA_BLOCK), MOBA_BLOCK)

    def tile_scores(tile, q_ref):
        q_maps = _split_rows(q_ref[...])

        def past_score(g, b, h):
            return jnp.dot(k_ref[block_rows(g, b), :], q_maps[h], preferred_element_type=f32)

        def own_score(b, h):
            group = tile * DIFF_TILE // KEY_GROUP
            key, query = _positions((group * GROUP_BLOCKS + b) * MOBA_BLOCK, MOBA_BLOCK, tile, DIFF_TILE)
            return jnp.where(key <= query, past_score(group, b, h), NEG)

        return own_score, past_score

    def value(g, b, h):
        return vt_ref[:, block_rows(g, b)]

    own_score, past_score = tile_scores(t, qt_ref)
    next_own_score, _ = tile_scores(t_next, qn_ref)
    _flash_pipeline(2, t == 0, t * DIFF_TILE // KEY_GROUP, own_score, past_score, next_own_score, value,
                    (_Slot(sa_ref, mxa_ref), _Slot(sb_ref, mxb_ref)), (m_ref, acc_ref))

    lam = (jnp.exp(jnp.sum(lq1_ref[...] * lk1_ref[...], axis=-1, keepdims=True))
           - jnp.exp(jnp.sum(lq2_ref[...] * lk2_ref[...], axis=-1, keepdims=True)) + lam_init)
    maps = [acc_ref[h, :LANES, :] / acc_ref[h, LANES:LANES + 1, :] for h in range(2)]
    out_t = maps[0] - lam * maps[1]
    out = _rms(out_t.T, sub_ref[...]) * (1.0 - lam_init)
    o_ref[...] = out.astype(bf16)


def _diff(nat, q_t, v_t, lq1, lk1, lq2, lk2, subln, lam_init):
    s = nat.shape[0]
    vec = _const_spec((1, HEAD_DIM))
    last = s // DIFF_TILE - 1
    return pl.pallas_call(
        functools.partial(_diff_kernel, lam_init),
        grid=(4, s // DIFF_TILE),
        in_specs=[pl.BlockSpec((LANES, DIFF_TILE), lambda h, t: (2 + h, t)),
                  pl.BlockSpec((LANES, DIFF_TILE), lambda h, t: (2 + h, jnp.minimum(t + 1, last))),
                  pl.BlockSpec((s, LANES), lambda h, t: (0, NAT_KC + h)),
                  pl.BlockSpec((LANES, s), lambda h, t: (2 + h, 0)),
                  vec, vec, vec, vec, _const_spec((1, LANES))],
        out_specs=pl.BlockSpec((DIFF_TILE, LANES), lambda h, t: (t, h)),
        out_shape=jax.ShapeDtypeStruct((s, 4 * LANES), bf16),
        scratch_shapes=_flash_scratch(2, LANES, DIFF_TILE),
        compiler_params=_params(2),
        name="diff_attention",
    )(q_t, q_t, nat, v_t, lq1, lk1, lq2, lk2, subln)


def _merge_kernel(x_ref, g_ref, wgate_ref, oa_ref, ob_ref, oc_ref, pa_ref, pb_ref, pc_ref, wo_ref, o_ref):
    x = x_ref[...]
    h = _rms(x, g_ref[...]).astype(bf16)
    branches = ((oa_ref, pa_ref), (ob_ref, pb_ref), (oc_ref, pc_ref))
    merged = jnp.zeros_like(x)
    for i, (b_ref, p_ref) in enumerate(branches):
        logits = jnp.dot(h, wgate_ref[:, i * D_MODEL:(i + 1) * D_MODEL], preferred_element_type=f32)
        y = jnp.dot(b_ref[...], p_ref[...], preferred_element_type=f32)
        merged = merged + jax.nn.sigmoid(logits) * y
    o_ref[...] = x + jnp.dot(merged.astype(bf16), wo_ref[...], preferred_element_type=f32)


def _merge(x, gain, w_gate, oa, ob, oc, pa, pb, pc, w_out, layer):
    s = x.shape[0]

    def rows(width):
        return pl.BlockSpec((ROW_TILE, width), lambda i: (i, 0))

    return pl.pallas_call(
        _merge_kernel,
        grid=(s // ROW_TILE,),
        in_specs=[rows(D_MODEL), _const_spec((1, D_MODEL)), _const_spec((D_MODEL, GATE_COLS)),
                  rows(oa.shape[1]), rows(ob.shape[1]), rows(oc.shape[1]),
                  _layer_spec(pa.shape[1:], layer), _layer_spec(pb.shape[1:], layer),
                  _layer_spec(pc.shape[1:], layer), _layer_spec((D_MODEL, D_MODEL), layer)],
        out_specs=rows(D_MODEL),
        out_shape=jax.ShapeDtypeStruct(x.shape, f32),
        compiler_params=_params(1),
        name="gated_merge",
    )(x, gain, w_gate, oa, ob, oc, pa, pb, pc, w_out)


def _rope_tables(seq):
    pos = jnp.arange(seq, dtype=f32)
    inv_freq = ROPE_THETA ** (-jnp.arange(0, HEAD_DIM, 2, dtype=f32) / HEAD_DIM)
    ang = pos[:, None] * inv_freq[None, :]
    cos, sin = jnp.cos(ang), jnp.sin(ang)
    return (jnp.tile(cos, (1, 4)), jnp.concatenate([-sin, sin, -sin, sin], axis=-1), cos.T, sin.T)


def _layer_weights(w_in):
    qa, ka, va = w_in[:, 0:256], w_in[:, 256:512], w_in[:, 512:768]
    qb, kb, vb = w_in[:, 768:1024], w_in[:, 1024:1152], w_in[:, 1152:1280]
    qc, kc, vc = w_in[:, 1280:1792], w_in[:, 1792:2304], w_in[:, 2304:2816]
    gates = w_in[:, 2816:]

    def dup(w):
        h0, h1 = w[:, :HEAD_DIM], w[:, HEAD_DIM:]
        return jnp.concatenate([h0, h0, h1, h1], axis=1)

    w_nat = jnp.concatenate([ka, kc, qb, dup(kb), dup(vb)], axis=1)
    w_t = jnp.concatenate([qa, qc, va, vc], axis=1).T
    return w_nat, w_t, gates


def _head_gains(qa, ka, qb, kb, qc, kc):
    scale = HEAD_DIM ** -0.5
    scale2 = scale * math.log2(math.e)
    nat = jnp.concatenate([jnp.tile(ka, 4), jnp.tile(kc, 8), jnp.tile(qb * scale, 4), jnp.tile(kb, 4)])
    feat = jnp.concatenate([jnp.tile(qa * scale2, 4), jnp.tile(qc * scale2, 8)])
    return nat[None, :], feat[:, None]


def kernel(x, ffn1_norm, ffn1_w_gate, ffn1_w_up, ffn1_w_down, mix_norm, w_in, moba_q_norm, moba_k_norm, swa_q_norm, swa_k_norm, swa_sinks, diff_q_norm, diff_k_norm, diff_lambda_q1, diff_lambda_k1, diff_lambda_q2, diff_lambda_k2, diff_subln, w_branch_a, w_branch_b, w_branch_c, w_out, ffn2_norm, ffn2_w_gate, ffn2_w_up, ffn2_w_down):
    batch, seq, _ = x.shape
    assert batch == 1 and seq % SWA_TILE == 0 and seq % ROW_TILE == 0 and seq % KEY_GROUP == 0
    depth = w_in.shape[0]
    cos, sin, cos_t, sin_t = _rope_tables(seq)
    ffn1 = [w.astype(bf16) for w in (ffn1_w_gate, ffn1_w_up, ffn1_w_down)]
    ffn2 = [w.astype(bf16) for w in (ffn2_w_gate, ffn2_w_up, ffn2_w_down)]
    out_w = [w.astype(bf16) for w in (w_branch_a, w_branch_b, w_branch_c, w_out)]
    w_in = w_in.astype(bf16)
    xs = x[0]
    for l in range(depth):
        lam_init = 0.8 - 0.6 * math.exp(-0.3 * l)
        xs = _ffn(xs, ffn1_norm[l][None], *ffn1, l)
        w_nat, w_t, w_gate = _layer_weights(w_in[l])
        hg_nat, hg_t = _head_gains(moba_q_norm[l], moba_k_norm[l], swa_q_norm[l], swa_k_norm[l],
                                   diff_q_norm[l], diff_k_norm[l])
        nat, q_t, v_t, kmean = _proj(xs, mix_norm[l][None], w_nat, w_t, hg_nat, hg_t, cos, sin, cos_t, sin_t)
        oa = _moba(nat, q_t, v_t, kmean.reshape(seq // MOBA_BLOCK, 2 * LANES))
        ob = _swa(nat, swa_sinks[l])
        oc = _diff(nat, q_t, v_t, diff_lambda_q1[l][None], diff_lambda_k1[l][None], diff_lambda_q2[l][None],
                   diff_lambda_k2[l][None], diff_subln[l][None], lam_init)
        xs = _merge(xs, mix_norm[l][None], w_gate, oa, ob, oc, *out_w, l)
        xs = _ffn(xs, ffn2_norm[l][None], *ffn2, l)
    return xs[None]
```

```python
import functools
import math

import jax
import jax.numpy as jnp
from jax import lax
from jax.experimental import pallas as pl
from jax.experimental.pallas import tpu as pltpu

D_MODEL = 1024
D_FF = 2816
HEAD_DIM = 64
HALF = HEAD_DIM // 2
LANES = 128
MOBA_BLOCK = 256
MOBA_TOPK = 3
SWA_WINDOW = 128
N_BRANCH = 3
ROPE_THETA = 10000.0
EPS = 1e-6
NEG = -1e30
FFN_HALF = 0.5

NAT_QK_COLS = 1280
NAT_COLS = 1536
NAT_KA, NAT_KC, NAT_QB, NAT_KB, NAT_VB = 0, 2, 6, 8, 10
QT_ROWS = 768
VT_ROWS = 768
GATE_COLS = N_BRANCH * D_MODEL

ROW_TILE = 512
MOBA_TILE = 512
DIFF_TILE = 512
GROUP_BLOCKS = 4
KEY_GROUP = GROUP_BLOCKS * MOBA_BLOCK
STAGES_PER_TRIP = 4
PARK_LEAD = 1
SUM_ROWS = 16
SWA_TILE = 2048
VMEM_LIMIT = 48 * 1024 * 1024

f32 = jnp.float32
bf16 = jnp.bfloat16

_NT = (((1,), (1,)), ((), ()))


def _rms(x, gain):
    return x * lax.rsqrt(jnp.mean(x * x, axis=-1, keepdims=True) + EPS) * gain


def _params(n_axes):
    return pltpu.CompilerParams(
        dimension_semantics=("arbitrary",) * n_axes, vmem_limit_bytes=VMEM_LIMIT)


def _const_spec(shape):
    return pl.BlockSpec(shape, lambda *_: (0,) * len(shape), pipeline_mode=pl.Buffered(1))


def _ffn_kernel(x_ref, g_ref, wg_ref, wu_ref, wd_ref, o_ref):
    x = x_ref[...]
    h = _rms(x, g_ref[...]).astype(bf16)
    a = jnp.dot(h, wg_ref[...], preferred_element_type=f32)
    b = jnp.dot(h, wu_ref[...], preferred_element_type=f32)
    act = (a * jax.nn.sigmoid(a) * b).astype(bf16)
    o_ref[...] = x + FFN_HALF * jnp.dot(act, wd_ref[...], preferred_element_type=f32)


def _layer_spec(shape, layer):
    return pl.BlockSpec((None,) + tuple(shape), lambda *_: (layer,) + (0,) * len(shape),
                        pipeline_mode=pl.Buffered(1))


def _ffn(x, gain, wg, wu, wd, layer):
    s = x.shape[0]
    row = pl.BlockSpec((ROW_TILE, D_MODEL), lambda i: (i, 0))
    return pl.pallas_call(
        _ffn_kernel,
        grid=(s // ROW_TILE,),
        in_specs=[row, _const_spec((1, D_MODEL)), _layer_spec((D_MODEL, D_FF), layer),
                  _layer_spec((D_MODEL, D_FF), layer), _layer_spec((D_FF, D_MODEL), layer)],
        out_specs=row,
        out_shape=jax.ShapeDtypeStruct(x.shape, f32),
        compiler_params=_params(1),
        name="ffn",
    )(x, gain, wg, wu, wd)


def _proj_kernel(x_ref, g_ref, wn_ref, wt_ref, hg_ref, hgt_ref, cos_ref, sin_ref, cost_ref, sint_ref,
                 nat_ref, qt_ref, vt_ref, km_ref):
    x = x_ref[...]
    h = _rms(x, g_ref[...]).astype(bf16)

    proj = jnp.dot(h, wn_ref[...], preferred_element_type=f32)
    lane = lax.broadcasted_iota(jnp.int32, (1, LANES), 1)
    first_head = lane < HEAD_DIM
    first_half = (lane % HEAD_DIM) < HALF
    cos = cos_ref[...]
    sin = sin_ref[...]
    for c in range(NAT_QK_COLS // LANES):
        cols = slice(c * LANES, (c + 1) * LANES)
        y = proj[:, cols]
        sq = y * y
        ss0 = jnp.sum(jnp.where(first_head, sq, 0.0), axis=-1, keepdims=True)
        ss1 = jnp.sum(jnp.where(first_head, 0.0, sq), axis=-1, keepdims=True)
        inv = jnp.where(first_head, lax.rsqrt(ss0 / HEAD_DIM + EPS), lax.rsqrt(ss1 / HEAD_DIM + EPS))
        y = y * inv * hg_ref[:, cols]
        partner = jnp.where(first_half, pltpu.roll(y, LANES - HALF, 1), pltpu.roll(y, HALF, 1))
        y = y * cos + partner * sin
        nat_ref[:, cols] = y.astype(bf16)
        if c in (NAT_KA, NAT_KA + 1):
            for b in range(ROW_TILE // MOBA_BLOCK):
                blk = y[b * MOBA_BLOCK:(b + 1) * MOBA_BLOCK]
                km_ref[b, :, (c - NAT_KA) * LANES:(c - NAT_KA + 1) * LANES] = jnp.mean(blk, axis=0, keepdims=True)
    nat_ref[:, NAT_QK_COLS:] = proj[:, NAT_QK_COLS:].astype(bf16)

    proj_t = lax.dot_general(wt_ref[...], h, _NT, preferred_element_type=f32)
    cos_t = cost_ref[...]
    sin_t = sint_ref[...]
    for c in range(QT_ROWS // HEAD_DIM):
        rows = slice(c * HEAD_DIM, (c + 1) * HEAD_DIM)
        y = proj_t[rows, :]
        inv = lax.rsqrt(jnp.mean(y * y, axis=0, keepdims=True) + EPS)
        y = y * inv * hgt_ref[rows, :]
        y1, y2 = y[:HALF], y[HALF:]
        qt_ref[c * HEAD_DIM:c * HEAD_DIM + HALF, :] = (y1 * cos_t - y2 * sin_t).astype(bf16)
        qt_ref[c * HEAD_DIM + HALF:(c + 1) * HEAD_DIM, :] = (y2 * cos_t + y1 * sin_t).astype(bf16)
    vt_ref[...] = proj_t[QT_ROWS:, :].astype(bf16)


def _proj(x, gain, w_nat, w_t, hg_nat, hg_t, cos, sin, cos_t, sin_t):
    s = x.shape[0]
    nb = ROW_TILE // MOBA_BLOCK
    return pl.pallas_call(
        _proj_kernel,
        grid=(s // ROW_TILE,),
        in_specs=[pl.BlockSpec((ROW_TILE, D_MODEL), lambda i: (i, 0)),
                  _const_spec((1, D_MODEL)),
                  _const_spec((D_MODEL, NAT_COLS)),
                  _const_spec((QT_ROWS + VT_ROWS, D_MODEL)),
                  _const_spec((1, NAT_QK_COLS)),
                  _const_spec((QT_ROWS, 1)),
                  pl.BlockSpec((ROW_TILE, LANES), lambda i: (i, 0)),
                  pl.BlockSpec((ROW_TILE, LANES), lambda i: (i, 0)),
                  pl.BlockSpec((HALF, ROW_TILE), lambda i: (0, i)),
                  pl.BlockSpec((HALF, ROW_TILE), lambda i: (0, i))],
        out_specs=[pl.BlockSpec((ROW_TILE, NAT_COLS), lambda i: (i, 0)),
                   pl.BlockSpec((QT_ROWS, ROW_TILE), lambda i: (0, i)),
                   pl.BlockSpec((VT_ROWS, ROW_TILE), lambda i: (0, i)),
                   pl.BlockSpec((nb, 1, 2 * LANES), lambda i: (i, 0, 0))],
        out_shape=[jax.ShapeDtypeStruct((s, NAT_COLS), bf16),
                   jax.ShapeDtypeStruct((QT_ROWS, s), bf16),
                   jax.ShapeDtypeStruct((VT_ROWS, s), bf16),
                   jax.ShapeDtypeStruct((s // MOBA_BLOCK, 1, 2 * LANES), f32)],
        compiler_params=_params(1),
        name="qkv_proj",
    )(x, gain, w_nat, w_t, hg_nat, hg_t, cos, sin, cos_t, sin_t)


def _split_rows(q_t):
    zero = jnp.zeros((HEAD_DIM, q_t.shape[1]), q_t.dtype)
    return (jnp.concatenate([q_t[:HEAD_DIM], zero], axis=0),
            jnp.concatenate([zero, q_t[HEAD_DIM:]], axis=0))


class _Slot:
    def __init__(self, s_ref, mx_ref):
        self.s, self.mx = s_ref, mx_ref


def _with_ones(v_t):
    return jnp.concatenate([v_t, jnp.ones((SUM_ROWS, v_t.shape[1]), v_t.dtype)], axis=0)


def _flash_stage(n_maps, state, score=None, nxt=None, cur=None, value=None):
    m_ref, acc_ref = state
    maps = range(n_maps)
    if cur is not None:
        m_old = [m_ref[h] for h in maps]
        m_new = [jnp.maximum(m_old[h], cur.mx[h]) for h in maps]
        for h in maps:
            acc_ref[h] = jnp.exp2(m_old[h] - m_new[h]) * acc_ref[h]
            m_ref[h] = m_new[h]
    col_max = [None] * n_maps

    def park(b):
        rows = slice(b * MOBA_BLOCK, (b + 1) * MOBA_BLOCK)
        for h in maps:
            s_b = score(b, h)
            nxt.s[h, rows, :] = s_b
            top = jnp.max(s_b, axis=0, keepdims=True)
            col_max[h] = top if b == 0 else jnp.maximum(col_max[h], top)
        if b == GROUP_BLOCKS - 1:
            for h in maps:
                nxt.mx[h] = col_max[h]

    def consume(b):
        rows = slice(b * MOBA_BLOCK, (b + 1) * MOBA_BLOCK)
        for h in maps:
            p = jnp.exp2(cur.s[h, rows, :] - m_new[h]).astype(bf16)
            acc_ref[h] += jnp.dot(_with_ones(value(b, h)), p, preferred_element_type=f32)

    if cur is not None and nxt is cur:
        for b in range(GROUP_BLOCKS):
            consume(b)
            park(b)
        return
    for step in range(GROUP_BLOCKS + PARK_LEAD):
        if score is not None and step < GROUP_BLOCKS:
            park(step)
        if cur is not None and step >= PARK_LEAD:
            consume(step - PARK_LEAD)


def _flash_pipeline(n_maps, first_tile, own_group, own_score, past_score, next_own_score, value, slots, state):
    m_ref, acc_ref = state
    m_ref[...] = jnp.full(m_ref.shape, NEG, f32)
    acc_ref[...] = jnp.zeros(acc_ref.shape, f32)

    def stage(base, offset, last):
        i = base + offset
        cur = slots[offset % 2]
        nxt = slots[0] if last else slots[1 - offset % 2]
        group = jnp.where(i == 0, own_group, i - 1)
        score = next_own_score if last else (lambda b, h: past_score(i, b, h))
        _flash_stage(n_maps, state, score=score, nxt=nxt, cur=cur, value=lambda b, h: value(group, b, h))

    pl.when(first_tile)(functools.partial(_flash_stage, n_maps, state, score=own_score, nxt=slots[0]))

    def trip(k, carry):
        for offset in range(STAGES_PER_TRIP):
            stage(k * STAGES_PER_TRIP, offset, False)
        return carry

    full_trips = own_group // STAGES_PER_TRIP
    lax.fori_loop(0, full_trips, trip, 0)

    base = full_trips * STAGES_PER_TRIP
    left = own_group - base
    for offset in range(STAGES_PER_TRIP):
        if offset > 0:
            pl.when(left >= offset)(functools.partial(stage, base, offset - 1, False))
        pl.when(left == offset)(functools.partial(stage, base, offset, True))


def _flash_scratch(n_maps, dv, tq):
    row = pltpu.VMEM((n_maps, 1, tq), f32)
    scores = pltpu.VMEM((n_maps, KEY_GROUP, tq), f32)
    return [row, pltpu.VMEM((n_maps, dv + SUM_ROWS, tq), f32), scores, row, scores, row]


def _positions(first_key, n_keys, t, tq):
    key = first_key + lax.broadcasted_iota(jnp.int32, (n_keys, tq), 0)
    query = t * tq + lax.broadcasted_iota(jnp.int32, (n_keys, tq), 1)
    return key, query


def _moba_kernel(qt_ref, qn_ref, k_ref, vt_ref, km_ref, o_ref, m_ref, acc_ref, sa_ref, mxa_ref, sb_ref, mxb_ref,
                 picks_ref):
    t = pl.program_id(1)
    t_next = jnp.minimum(t + 1, pl.num_programs(1) - 1)
    n_blocks = km_ref.shape[0]

    km = km_ref[...]
    km_hi = km.astype(bf16)
    rem = km - km_hi.astype(f32)
    km_mid = rem.astype(bf16)
    km_lo = (rem - km_mid.astype(f32)).astype(bf16)
    blk = lax.broadcasted_iota(jnp.int32, (n_blocks, 1), 0).astype(f32)

    def block_rows(g, b):
        return pl.ds(pl.multiple_of((g * GROUP_BLOCKS + b) * MOBA_BLOCK, MOBA_BLOCK), MOBA_BLOCK)

    def query_row(tile):
        return tile * MOBA_TILE + lax.broadcasted_iota(jnp.int32, (1, MOBA_TILE), 1)

    def pick_blocks(tile, q_heads):
        own_blk = (query_row(tile) // MOBA_BLOCK).astype(f32)
        picks = []
        for qh in q_heads:
            gate = (jnp.dot(km_hi, qh, preferred_element_type=f32)
                    + jnp.dot(km_mid, qh, preferred_element_type=f32)
                    + jnp.dot(km_lo, qh, preferred_element_type=f32))
            gate = jnp.where(blk < own_blk, gate, NEG)
            chosen = []
            for _ in range(MOBA_TOPK):
                best = jnp.max(gate, axis=0, keepdims=True)
                idx = jnp.min(jnp.where(gate == best, blk, 1e9), axis=0, keepdims=True)
                chosen.append(jnp.where(best > 0.5 * NEG, idx, -1.0))
                gate = jnp.where(blk == idx, 2.0 * NEG, gate)
            picks.append(chosen)
        return picks

    def tile_scores(tile, q_heads, picks):
        def picked(g, b, h):
            blk_f = (g * GROUP_BLOCKS + b).astype(f32)
            return (picks[h][0] == blk_f) | (picks[h][1] == blk_f) | (picks[h][2] == blk_f)

        def past_score(g, b, h):
            bias = jnp.where(picked(g, b, h), 0.0, NEG)
            first_row = lax.broadcasted_iota(jnp.int32, (SUM_ROWS, MOBA_TILE), 0) == 0
            bias_rows = jnp.where(first_row, bias, 0.0).astype(bf16)
            rest = jnp.zeros((LANES - SUM_ROWS, MOBA_TILE), bf16)
            rhs = jnp.concatenate([q_heads[h], bias_rows, rest], axis=0)
            lhs = jnp.concatenate([k_ref[block_rows(g, b), :], jnp.ones((MOBA_BLOCK, LANES), bf16)], axis=1)
            return jnp.dot(lhs, rhs, preferred_element_type=f32)

        def own_score(b, h):
            group = tile * MOBA_TILE // KEY_GROUP
            s_b = jnp.dot(k_ref[block_rows(group, b), :], q_heads[h], preferred_element_type=f32)
            key, query = _positions((group * GROUP_BLOCKS + b) * MOBA_BLOCK, MOBA_BLOCK, tile, MOBA_TILE)
            own_start = query_row(tile) // MOBA_BLOCK * MOBA_BLOCK
            masked = jnp.where(picked(group, b, h), s_b, NEG)
            return jnp.where(key >= own_start, jnp.where(key <= query, s_b, NEG), masked)

        return own_score, past_score

    def value(g, b, h):
        return vt_ref[h * HEAD_DIM:(h + 1) * HEAD_DIM, block_rows(g, b)]

    def store_picks(picks):
        for h in range(2):
            for i in range(MOBA_TOPK):
                picks_ref[h * MOBA_TOPK + i:h * MOBA_TOPK + i + 1, :] = picks[h][i]

    q_heads = _split_rows(qt_ref[...])
    pl.when(t == 0)(lambda: store_picks(pick_blocks(t, q_heads)))
    picks = [[picks_ref[h * MOBA_TOPK + i:h * MOBA_TOPK + i + 1, :] for i in range(MOBA_TOPK)] for h in range(2)]
    q_next = _split_rows(qn_ref[...])
    picks_next = pick_blocks(t_next, q_next)
    own_score, past_score = tile_scores(t, q_heads, picks)
    next_own_score, _ = tile_scores(t_next, q_next, picks_next)
    _flash_pipeline(2, t == 0, t * MOBA_TILE // KEY_GROUP, own_score, past_score, next_own_score, value,
                    (_Slot(sa_ref, mxa_ref), _Slot(sb_ref, mxb_ref)), (m_ref, acc_ref))
    store_picks(picks_next)

    heads = [acc_ref[h, :HEAD_DIM, :] * (1.0 / acc_ref[h, HEAD_DIM:HEAD_DIM + 1, :]) for h in range(2)]
    o_ref[...] = jnp.concatenate(heads, axis=0).T.astype(bf16)


def _moba(nat, q_t, v_t, kmean):
    s = nat.shape[0]
    n_blocks = s // MOBA_BLOCK
    last = s // MOBA_TILE - 1
    return pl.pallas_call(
        _moba_kernel,
        grid=(2, s // MOBA_TILE),
        in_specs=[pl.BlockSpec((LANES, MOBA_TILE), lambda p, t: (p, t)),
                  pl.BlockSpec((LANES, MOBA_TILE), lambda p, t: (p, jnp.minimum(t + 1, last))),
                  pl.BlockSpec((s, LANES), lambda p, t: (0, NAT_KA + p)),
                  pl.BlockSpec((LANES, s), lambda p, t: (p, 0)),
                  pl.BlockSpec((n_blocks, LANES), lambda p, t: (0, p))],
        out_specs=pl.BlockSpec((MOBA_TILE, LANES), lambda p, t: (t, p)),
        out_shape=jax.ShapeDtypeStruct((s, 2 * LANES), bf16),
        scratch_shapes=_flash_scratch(2, HEAD_DIM, MOBA_TILE) + [pltpu.VMEM((2 * MOBA_TOPK, MOBA_TILE), f32)],
        compiler_params=_params(2),
        name="moba_attention",
    )(q_t, q_t, nat, v_t, kmean)


def _swa_kernel(sink_ref, q_ref, k_ref, kp_ref, v_ref, vp_ref, o_ref):
    p_id = pl.program_id(0)
    t = pl.program_id(1)
    sub_tiles = SWA_TILE // SWA_WINDOW
    lane = lax.broadcasted_iota(jnp.int32, (1, LANES), 1)
    zero = jnp.zeros((SWA_WINDOW, LANES), bf16)
    qi = lax.broadcasted_iota(jnp.int32, (SWA_WINDOW, 2 * SWA_WINDOW), 0)
    kj = lax.broadcasted_iota(jnp.int32, (SWA_WINDOW, 2 * SWA_WINDOW), 1)
    back = qi + SWA_WINDOW - kj
    in_window = (back >= 0) & (back < SWA_WINDOW)
    for sub in range(sub_tiles):
        rows = slice(sub * SWA_WINDOW, (sub + 1) * SWA_WINDOW)
        if sub == 0:
            k_prev, v_prev = kp_ref[...], vp_ref[...]
            mask = in_window & (kj >= SWA_WINDOW - t * SWA_TILE)
        else:
            prev = slice((sub - 1) * SWA_WINDOW, sub * SWA_WINDOW)
            k_prev, v_prev = k_ref[prev, :], v_ref[prev, :]
            mask = in_window
        k_band = jnp.concatenate([k_prev, k_ref[rows, :]], axis=0)
        v_band = jnp.concatenate([v_prev, v_ref[rows, :]], axis=0)
        q = q_ref[rows, :]
        outs = []
        for h, qh in enumerate((jnp.where(lane < HEAD_DIM, q, zero), jnp.where(lane < HEAD_DIM, zero, q))):
            sink = sink_ref[2 * p_id + h]
            s = lax.dot_general(qh, k_band, _NT, preferred_element_type=f32)
            s = jnp.where(mask, s, NEG)
            m = jnp.maximum(jnp.max(s, axis=-1, keepdims=True), sink)
            p = jnp.exp(s - m)
            denom = jnp.sum(p, axis=-1, keepdims=True) + jnp.exp(sink - m)
            outs.append(jnp.dot(p.astype(bf16), v_band, preferred_element_type=f32) / denom)
        o_ref[rows, :] = jnp.where(lane < HEAD_DIM, outs[0], outs[1]).astype(bf16)


def _swa(nat, sinks):
    s = nat.shape[0]
    ratio = SWA_TILE // SWA_WINDOW

    def prev_rows(t):
        return jnp.maximum(t * ratio - 1, 0)

    return pl.pallas_call(
        _swa_kernel,
        grid=(2, s // SWA_TILE),
        in_specs=[pl.BlockSpec(memory_space=pltpu.SMEM),
                  pl.BlockSpec((SWA_TILE, LANES), lambda p, t: (t, NAT_QB + p)),
                  pl.BlockSpec((SWA_TILE, LANES), lambda p, t: (t, NAT_KB + p)),
                  pl.BlockSpec((SWA_WINDOW, LANES), lambda p, t: (prev_rows(t), NAT_KB + p)),
                  pl.BlockSpec((SWA_TILE, LANES), lambda p, t: (t, NAT_VB + p)),
                  pl.BlockSpec((SWA_WINDOW, LANES), lambda p, t: (prev_rows(t), NAT_VB + p))],
        out_specs=pl.BlockSpec((SWA_TILE, LANES), lambda p, t: (t, p)),
        out_shape=jax.ShapeDtypeStruct((s, 2 * LANES), bf16),
        compiler_params=_params(2),
        name="swa_attention",
    )(sinks, nat, nat, nat, nat, nat)


def _diff_kernel(lam_init, qt_ref, qn_ref, k_ref, vt_ref, lq1_ref, lk1_ref, lq2_ref, lk2_ref, sub_ref,
                 o_ref, m_ref, acc_ref, sa_ref, mxa_ref, sb_ref, mxb_ref):
    t = pl.program_id(1)
    t_next = jnp.minimum(t + 1, pl.num_programs(1) - 1)

    def block_rows(g, b):
        return pl.ds(pl.multiple_of((g * GROUP_BLOCKS + b) * MOBA_BLOCK, MOBA_BLOCK), MOBA_BLOCK)

    def tile_scores(tile, q_ref):
        q_maps = _split_rows(q_ref[...])

        def past_score(g, b, h):
            return jnp.dot(k_ref[block_rows(g, b), :], q_maps[h], preferred_element_type=f32)

        def own_score(b, h):
            group = tile * DIFF_TILE // KEY_GROUP
            key, query = _positions((group * GROUP_BLOCKS + b) * MOBA_BLOCK, MOBA_BLOCK, tile, DIFF_TILE)
            return jnp.where(key <= query, past_score(group, b, h), NEG)

        return own_score, past_score

    def value(g, b, h):
        return vt_ref[:, block_rows(g, b)]

    own_score, past_score = tile_scores(t, qt_ref)
    next_own_score, _ = tile_scores(t_next, qn_ref)
    _flash_pipeline(2, t == 0, t * DIFF_TILE // KEY_GROUP, own_score, past_score, next_own_score, value,
                    (_Slot(sa_ref, mxa_ref), _Slot(sb_ref, mxb_ref)), (m_ref, acc_ref))

    lam = (jnp.exp(jnp.sum(lq1_ref[...] * lk1_ref[...], axis=-1, keepdims=True))
           - jnp.exp(jnp.sum(lq2_ref[...] * lk2_ref[...], axis=-1, keepdims=True)) + lam_init)
    maps = [acc_ref[h, :LANES, :] * (1.0 / acc_ref[h, LANES:LANES + 1, :]) for h in range(2)]
    out_t = maps[0] - lam * maps[1]
    inv = lax.rsqrt(jnp.mean(out_t * out_t, axis=0, keepdims=True) + EPS) * (1.0 - lam_init)
    o_ref[...] = ((out_t * inv).T * sub_ref[...]).astype(bf16)


def _diff(nat, q_t, v_t, lq1, lk1, lq2, lk2, subln, lam_init):
    s = nat.shape[0]
    vec = _const_spec((1, HEAD_DIM))
    last = s // DIFF_TILE - 1
    return pl.pallas_call(
        functools.partial(_diff_kernel, lam_init),
        grid=(4, s // DIFF_TILE),
        in_specs=[pl.BlockSpec((LANES, DIFF_TILE), lambda h, t: (2 + h, t)),
                  pl.BlockSpec((LANES, DIFF_TILE), lambda h, t: (2 + h, jnp.minimum(t + 1, last))),
                  pl.BlockSpec((s, LANES), lambda h, t: (0, NAT_KC + h)),
                  pl.BlockSpec((LANES, s), lambda h, t: (2 + h, 0)),
                  vec, vec, vec, vec, _const_spec((1, LANES))],
        out_specs=pl.BlockSpec((DIFF_TILE, LANES), lambda h, t: (t, h)),
        out_shape=jax.ShapeDtypeStruct((s, 4 * LANES), bf16),
        scratch_shapes=_flash_scratch(2, LANES, DIFF_TILE),
        compiler_params=_params(2),
        name="diff_attention",
    )(q_t, q_t, nat, v_t, lq1, lk1, lq2, lk2, subln)


def _merge_kernel(x_ref, g_ref, wgate_ref, oa_ref, ob_ref, oc_ref, pa_ref, pb_ref, pc_ref, wo_ref, o_ref):
    x = x_ref[...]
    h = _rms(x, g_ref[...]).astype(bf16)
    branches = ((oa_ref, pa_ref), (ob_ref, pb_ref), (oc_ref, pc_ref))
    merged = jnp.zeros_like(x)
    for i, (b_ref, p_ref) in enumerate(branches):
        logits = jnp.dot(h, wgate_ref[:, i * D_MODEL:(i + 1) * D_MODEL], preferred_element_type=f32)
        y = jnp.dot(b_ref[...], p_ref[...], preferred_element_type=f32)
        merged = merged + jax.nn.sigmoid(logits) * y
    o_ref[...] = x + jnp.dot(merged.astype(bf16), wo_ref[...], preferred_element_type=f32)


def _merge(x, gain, w_gate, oa, ob, oc, pa, pb, pc, w_out, layer):
    s = x.shape[0]

    def rows(width):
        return pl.BlockSpec((ROW_TILE, width), lambda i: (i, 0))

    return pl.pallas_call(
        _merge_kernel,
        grid=(s // ROW_TILE,),
        in_specs=[rows(D_MODEL), _const_spec((1, D_MODEL)), _const_spec((D_MODEL, GATE_COLS)),
                  rows(oa.shape[1]), rows(ob.shape[1]), rows(oc.shape[1]),
                  _layer_spec(pa.shape[1:], layer), _layer_spec(pb.shape[1:], layer),
                  _layer_spec(pc.shape[1:], layer), _layer_spec((D_MODEL, D_MODEL), layer)],
        out_specs=rows(D_MODEL),
        out_shape=jax.ShapeDtypeStruct(x.shape, f32),
        compiler_params=_params(1),
        name="gated_merge",
    )(x, gain, w_gate, oa, ob, oc, pa, pb, pc, w_out)


def _rope_tables(seq):
    pos = jnp.arange(seq, dtype=f32)
    inv_freq = ROPE_THETA ** (-jnp.arange(0, HEAD_DIM, 2, dtype=f32) / HEAD_DIM)
    ang = pos[:, None] * inv_freq[None, :]
    cos, sin = jnp.cos(ang), jnp.sin(ang)
    return (jnp.tile(cos, (1, 4)), jnp.concatenate([-sin, sin, -sin, sin], axis=-1), cos.T, sin.T)


def _layer_weights(w_in):
    qa, ka, va = w_in[:, 0:256], w_in[:, 256:512], w_in[:, 512:768]
    qb, kb, vb = w_in[:, 768:1024], w_in[:, 1024:1152], w_in[:, 1152:1280]
    qc, kc, vc = w_in[:, 1280:1792], w_in[:, 1792:2304], w_in[:, 2304:2816]
    gates = w_in[:, 2816:]

    def dup(w):
        h0, h1 = w[:, :HEAD_DIM], w[:, HEAD_DIM:]
        return jnp.concatenate([h0, h0, h1, h1], axis=1)

    w_nat = jnp.concatenate([ka, kc, qb, dup(kb), dup(vb)], axis=1)
    w_t = jnp.concatenate([qa, qc, va, vc], axis=1).T
    return w_nat, w_t, gates


def _head_gains(qa, ka, qb, kb, qc, kc):
    scale = HEAD_DIM ** -0.5
    scale2 = scale * math.log2(math.e)
    nat = jnp.concatenate([jnp.tile(ka, 4), jnp.tile(kc, 8), jnp.tile(qb * scale, 4), jnp.tile(kb, 4)])
    feat = jnp.concatenate([jnp.tile(qa * scale2, 4), jnp.tile(qc * scale2, 8)])
    return nat[None, :], feat[:, None]


def kernel(x, ffn1_norm, ffn1_w_gate, ffn1_w_up, ffn1_w_down, mix_norm, w_in, moba_q_norm, moba_k_norm, swa_q_norm, swa_k_norm, swa_sinks, diff_q_norm, diff_k_norm, diff_lambda_q1, diff_lambda_k1, diff_lambda_q2, diff_lambda_k2, diff_subln, w_branch_a, w_branch_b, w_branch_c, w_out, ffn2_norm, ffn2_w_gate, ffn2_w_up, ffn2_w_down):
    batch, seq, _ = x.shape
    assert batch == 1 and seq % SWA_TILE == 0 and seq % ROW_TILE == 0 and seq % KEY_GROUP == 0
    depth = w_in.shape[0]
    cos, sin, cos_t, sin_t = _rope_tables(seq)
    ffn1 = [w.astype(bf16) for w in (ffn1_w_gate, ffn1_w_up, ffn1_w_down)]
    ffn2 = [w.astype(bf16) for w in (ffn2_w_gate, ffn2_w_up, ffn2_w_down)]
    out_w = [w.astype(bf16) for w in (w_branch_a, w_branch_b, w_branch_c, w_out)]
    w_in = w_in.astype(bf16)
    xs = x[0]
    for l in range(depth):
        lam_init = 0.8 - 0.6 * math.exp(-0.3 * l)
        xs = _ffn(xs, ffn1_norm[l][None], *ffn1, l)
        w_nat, w_t, w_gate = _layer_weights(w_in[l])
        hg_nat, hg_t = _head_gains(moba_q_norm[l], moba_k_norm[l], swa_q_norm[l], swa_k_norm[l],
                                   diff_q_norm[l], diff_k_norm[l])
        nat, q_t, v_t, kmean = _proj(xs, mix_norm[l][None], w_nat, w_t, hg_nat, hg_t, cos, sin, cos_t, sin_t)
        oa = _moba(nat, q_t, v_t, kmean.reshape(seq // MOBA_BLOCK, 2 * LANES))
        ob = _swa(nat, swa_sinks[l])
        oc = _diff(nat, q_t, v_t, diff_lambda_q1[l][None], diff_lambda_k1[l][None], diff_lambda_q2[l][None],
                   diff_lambda_k2[l][None], diff_subln[l][None], lam_init)
        xs = _merge(xs, mix_norm[l][None], w_gate, oa, ob, oc, *out_w, l)
        xs = _ffn(xs, ffn2_norm[l][None], *ffn2, l)
    return xs[None]
```

```python
import functools
import math

import jax
import jax.numpy as jnp
from jax import lax
from jax.experimental import pallas as pl
from jax.experimental.pallas import tpu as pltpu

D_MODEL = 1024
D_FF = 2816
HEAD_DIM = 64
HALF = HEAD_DIM // 2
LANES = 128
MOBA_BLOCK = 256
MOBA_TOPK = 3
SWA_WINDOW = 128
N_BRANCH = 3
ROPE_THETA = 10000.0
EPS = 1e-6
NEG = -1e30
FFN_HALF = 0.5

NAT_QK_COLS = 1280
NAT_COLS = 1536
NAT_KA, NAT_KC, NAT_QB, NAT_KB, NAT_VB = 0, 2, 6, 8, 10
QT_ROWS = 768
VT_ROWS = 768
GATE_COLS = N_BRANCH * D_MODEL

ROW_TILE = 512
MOBA_TILE = 512
DIFF_TILE = 512
GROUP_BLOCKS = 4
KEY_GROUP = GROUP_BLOCKS * MOBA_BLOCK
STAGES_PER_TRIP = 4
PARK_LEAD = 1
SUM_ROWS = 16
SWA_TILE = 2048
VMEM_LIMIT = 48 * 1024 * 1024

f32 = jnp.float32
bf16 = jnp.bfloat16

_NT = (((1,), (1,)), ((), ()))


def _rms(x, gain):
    return x * lax.rsqrt(jnp.mean(x * x, axis=-1, keepdims=True) + EPS) * gain


def _params(n_axes):
    return pltpu.CompilerParams(
        dimension_semantics=("arbitrary",) * n_axes, vmem_limit_bytes=VMEM_LIMIT)


def _const_spec(shape):
    return pl.BlockSpec(shape, lambda *_: (0,) * len(shape), pipeline_mode=pl.Buffered(1))


def _ffn_kernel(x_ref, g_ref, wg_ref, wu_ref, wd_ref, o_ref):
    x = x_ref[...]
    h = _rms(x, g_ref[...]).astype(bf16)
    a = jnp.dot(h, wg_ref[...], preferred_element_type=f32)
    b = jnp.dot(h, wu_ref[...], preferred_element_type=f32)
    act = (a * jax.nn.sigmoid(a) * b).astype(bf16)
    o_ref[...] = x + FFN_HALF * jnp.dot(act, wd_ref[...], preferred_element_type=f32)


def _layer_spec(shape, layer):
    return pl.BlockSpec((None,) + tuple(shape), lambda *_: (layer,) + (0,) * len(shape),
                        pipeline_mode=pl.Buffered(1))


def _ffn(x, gain, wg, wu, wd, layer):
    s = x.shape[0]
    row = pl.BlockSpec((ROW_TILE, D_MODEL), lambda i: (i, 0))
    return pl.pallas_call(
        _ffn_kernel,
        grid=(s // ROW_TILE,),
        in_specs=[row, _const_spec((1, D_MODEL)), _layer_spec((D_MODEL, D_FF), layer),
                  _layer_spec((D_MODEL, D_FF), layer), _layer_spec((D_FF, D_MODEL), layer)],
        out_specs=row,
        out_shape=jax.ShapeDtypeStruct(x.shape, f32),
        compiler_params=_params(1),
        name="ffn",
    )(x, gain, wg, wu, wd)


def _proj_kernel(x_ref, g_ref, wn_ref, wt_ref, hg_ref, hgt_ref, cos_ref, sin_ref, cost_ref, sint_ref,
                 nat_ref, qt_ref, vt_ref, km_ref):
    x = x_ref[...]
    h = _rms(x, g_ref[...]).astype(bf16)

    proj = jnp.dot(h, wn_ref[...], preferred_element_type=f32)
    lane = lax.broadcasted_iota(jnp.int32, (1, LANES), 1)
    first_head = lane < HEAD_DIM
    first_half = (lane % HEAD_DIM) < HALF
    cos = cos_ref[...]
    sin = sin_ref[...]
    for c in range(NAT_QK_COLS // LANES):
        cols = slice(c * LANES, (c + 1) * LANES)
        y = proj[:, cols]
        sq = y * y
        ss0 = jnp.sum(jnp.where(first_head, sq, 0.0), axis=-1, keepdims=True)
        ss1 = jnp.sum(jnp.where(first_head, 0.0, sq), axis=-1, keepdims=True)
        inv = jnp.where(first_head, lax.rsqrt(ss0 / HEAD_DIM + EPS), lax.rsqrt(ss1 / HEAD_DIM + EPS))
        y = y * inv * hg_ref[:, cols]
        partner = jnp.where(first_half, pltpu.roll(y, LANES - HALF, 1), pltpu.roll(y, HALF, 1))
        y = y * cos + partner * sin
        nat_ref[:, cols] = y.astype(bf16)
        if c in (NAT_KA, NAT_KA + 1):
            for b in range(ROW_TILE // MOBA_BLOCK):
                blk = y[b * MOBA_BLOCK:(b + 1) * MOBA_BLOCK]
                km_ref[b, :, (c - NAT_KA) * LANES:(c - NAT_KA + 1) * LANES] = jnp.mean(blk, axis=0, keepdims=True)
    nat_ref[:, NAT_QK_COLS:] = proj[:, NAT_QK_COLS:].astype(bf16)

    proj_t = lax.dot_general(wt_ref[...], h, _NT, preferred_element_type=f32)
    cos_t = cost_ref[...]
    sin_t = sint_ref[...]
    for c in range(QT_ROWS // HEAD_DIM):
        rows = slice(c * HEAD_DIM, (c + 1) * HEAD_DIM)
        y = proj_t[rows, :]
        inv = lax.rsqrt(jnp.mean(y * y, axis=0, keepdims=True) + EPS)
        y = y * inv * hgt_ref[rows, :]
        y1, y2 = y[:HALF], y[HALF:]
        qt_ref[c * HEAD_DIM:c * HEAD_DIM + HALF, :] = (y1 * cos_t - y2 * sin_t).astype(bf16)
        qt_ref[c * HEAD_DIM + HALF:(c + 1) * HEAD_DIM, :] = (y2 * cos_t + y1 * sin_t).astype(bf16)
    vt_ref[...] = proj_t[QT_ROWS:, :].astype(bf16)


def _proj(x, gain, w_nat, w_t, hg_nat, hg_t, cos, sin, cos_t, sin_t):
    s = x.shape[0]
    nb = ROW_TILE // MOBA_BLOCK
    return pl.pallas_call(
        _proj_kernel,
        grid=(s // ROW_TILE,),
        in_specs=[pl.BlockSpec((ROW_TILE, D_MODEL), lambda i: (i, 0)),
                  _const_spec((1, D_MODEL)),
                  _const_spec((D_MODEL, NAT_COLS)),
                  _const_spec((QT_ROWS + VT_ROWS, D_MODEL)),
                  _const_spec((1, NAT_QK_COLS)),
                  _const_spec((QT_ROWS, 1)),
                  pl.BlockSpec((ROW_TILE, LANES), lambda i: (i, 0)),
                  pl.BlockSpec((ROW_TILE, LANES), lambda i: (i, 0)),
                  pl.BlockSpec((HALF, ROW_TILE), lambda i: (0, i)),
                  pl.BlockSpec((HALF, ROW_TILE), lambda i: (0, i))],
        out_specs=[pl.BlockSpec((ROW_TILE, NAT_COLS), lambda i: (i, 0)),
                   pl.BlockSpec((QT_ROWS, ROW_TILE), lambda i: (0, i)),
                   pl.BlockSpec((VT_ROWS, ROW_TILE), lambda i: (0, i)),
                   pl.BlockSpec((nb, 1, 2 * LANES), lambda i: (i, 0, 0))],
        out_shape=[jax.ShapeDtypeStruct((s, NAT_COLS), bf16),
                   jax.ShapeDtypeStruct((QT_ROWS, s), bf16),
                   jax.ShapeDtypeStruct((VT_ROWS, s), bf16),
                   jax.ShapeDtypeStruct((s // MOBA_BLOCK, 1, 2 * LANES), f32)],
        compiler_params=_params(1),
        name="qkv_proj",
    )(x, gain, w_nat, w_t, hg_nat, hg_t, cos, sin, cos_t, sin_t)


def _split_rows(q_t):
    zero = jnp.zeros((HEAD_DIM, q_t.shape[1]), q_t.dtype)
    return (jnp.concatenate([q_t[:HEAD_DIM], zero], axis=0),
            jnp.concatenate([zero, q_t[HEAD_DIM:]], axis=0))


class _Slot:
    def __init__(self, s_ref, mx_ref):
        self.s, self.mx = s_ref, mx_ref


def _with_ones(v_t):
    return jnp.concatenate([v_t, jnp.ones((SUM_ROWS, v_t.shape[1]), v_t.dtype)], axis=0)


def _flash_stage(n_maps, state, score=None, nxt=None, cur=None, value=None):
    m_ref, acc_ref = state
    maps = range(n_maps)
    if cur is not None:
        m_old = [m_ref[h] for h in maps]
        m_new = [jnp.maximum(m_old[h], cur.mx[h]) for h in maps]
        for h in maps:
            acc_ref[h] = jnp.exp2(m_old[h] - m_new[h]) * acc_ref[h]
            m_ref[h] = m_new[h]
    col_max = [None] * n_maps

    def park(b):
        rows = slice(b * MOBA_BLOCK, (b + 1) * MOBA_BLOCK)
        for h in maps:
            s_b = score(b, h)
            nxt.s[h, rows, :] = s_b
            top = jnp.max(s_b, axis=0, keepdims=True)
            col_max[h] = top if b == 0 else jnp.maximum(col_max[h], top)
        if b == GROUP_BLOCKS - 1:
            for h in maps:
                nxt.mx[h] = col_max[h]

    def consume(b):
        rows = slice(b * MOBA_BLOCK, (b + 1) * MOBA_BLOCK)
        for h in maps:
            p = jnp.exp2(cur.s[h, rows, :] - m_new[h]).astype(bf16)
            acc_ref[h] += jnp.dot(_with_ones(value(b, h)), p, preferred_element_type=f32)

    if cur is not None and nxt is cur:
        for b in range(GROUP_BLOCKS):
            consume(b)
            park(b)
        return
    for step in range(GROUP_BLOCKS + PARK_LEAD):
        if score is not None and step < GROUP_BLOCKS:
            park(step)
        if cur is not None and step >= PARK_LEAD:
            consume(step - PARK_LEAD)


def _flash_pipeline(n_maps, first_tile, own_group, own_score, past_score, next_own, value, slots, state):
    m_ref, acc_ref = state
    m_ref[...] = jnp.full(m_ref.shape, NEG, f32)
    acc_ref[...] = jnp.zeros(acc_ref.shape, f32)

    def stage(base, offset, last):
        i = base + offset
        cur = slots[offset % 2]
        nxt = slots[0] if last else slots[1 - offset % 2]
        group = jnp.where(i == 0, own_group, i - 1)
        score = next_own() if last else (lambda b, h: past_score(i, b, h))
        _flash_stage(n_maps, state, score=score, nxt=nxt, cur=cur, value=lambda b, h: value(group, b, h))

    pl.when(first_tile)(functools.partial(_flash_stage, n_maps, state, score=own_score, nxt=slots[0]))

    def trip(k, carry):
        for offset in range(STAGES_PER_TRIP):
            stage(k * STAGES_PER_TRIP, offset, False)
        return carry

    full_trips = own_group // STAGES_PER_TRIP
    lax.fori_loop(0, full_trips, trip, 0)

    base = full_trips * STAGES_PER_TRIP
    left = own_group - base
    for offset in range(STAGES_PER_TRIP):
        if offset > 0:
            pl.when(left >= offset)(functools.partial(stage, base, offset - 1, False))
        pl.when(left == offset)(functools.partial(stage, base, offset, True))


def _flash_scratch(n_maps, dv, tq):
    row = pltpu.VMEM((n_maps, 1, tq), f32)
    scores = pltpu.VMEM((n_maps, KEY_GROUP, tq), f32)
    return [row, pltpu.VMEM((n_maps, dv + SUM_ROWS, tq), f32), scores, row, scores, row]


def _positions(first_key, n_keys, t, tq):
    key = first_key + lax.broadcasted_iota(jnp.int32, (n_keys, tq), 0)
    query = t * tq + lax.broadcasted_iota(jnp.int32, (n_keys, tq), 1)
    return key, query


def _moba_kernel(qt_ref, qn_ref, k_ref, vt_ref, km_ref, o_ref, m_ref, acc_ref, sa_ref, mxa_ref, sb_ref, mxb_ref,
                 picks_ref):
    t = pl.program_id(1)
    t_next = jnp.minimum(t + 1, pl.num_programs(1) - 1)
    n_blocks = km_ref.shape[0]

    km = km_ref[...]
    km_hi = km.astype(bf16)
    rem = km - km_hi.astype(f32)
    km_mid = rem.astype(bf16)
    km_lo = (rem - km_mid.astype(f32)).astype(bf16)
    blk = lax.broadcasted_iota(jnp.int32, (n_blocks, 1), 0).astype(f32)

    def block_rows(g, b):
        return pl.ds(pl.multiple_of((g * GROUP_BLOCKS + b) * MOBA_BLOCK, MOBA_BLOCK), MOBA_BLOCK)

    def query_row(tile):
        return tile * MOBA_TILE + lax.broadcasted_iota(jnp.int32, (1, MOBA_TILE), 1)

    def pick_blocks(tile, q_heads):
        own_blk = (query_row(tile) // MOBA_BLOCK).astype(f32)
        picks = []
        for qh in q_heads:
            gate = (jnp.dot(km_hi, qh, preferred_element_type=f32)
                    + jnp.dot(km_mid, qh, preferred_element_type=f32)
                    + jnp.dot(km_lo, qh, preferred_element_type=f32))
            gate = jnp.where(blk < own_blk, gate, NEG)
            chosen = []
            for _ in range(MOBA_TOPK):
                best = jnp.max(gate, axis=0, keepdims=True)
                idx = jnp.min(jnp.where(gate == best, blk, 1e9), axis=0, keepdims=True)
                chosen.append(jnp.where(best > 0.5 * NEG, idx, -1.0))
                gate = jnp.where(blk == idx, 2.0 * NEG, gate)
            picks.append(chosen)
        return picks

    def tile_scores(tile, q_heads, picks):
        def picked(g, b, h):
            blk_f = (g * GROUP_BLOCKS + b).astype(f32)
            return (picks[h][0] == blk_f) | (picks[h][1] == blk_f) | (picks[h][2] == blk_f)

        def past_score(g, b, h):
            bias = jnp.where(picked(g, b, h), 0.0, NEG)
            first_row = lax.broadcasted_iota(jnp.int32, (SUM_ROWS, MOBA_TILE), 0) == 0
            bias_rows = jnp.where(first_row, bias, 0.0).astype(bf16)
            rest = jnp.zeros((LANES - SUM_ROWS, MOBA_TILE), bf16)
            rhs = jnp.concatenate([q_heads[h], bias_rows, rest], axis=0)
            lhs = jnp.concatenate([k_ref[block_rows(g, b), :], jnp.ones((MOBA_BLOCK, LANES), bf16)], axis=1)
            return jnp.dot(lhs, rhs, preferred_element_type=f32)

        def own_score(b, h):
            group = tile * MOBA_TILE // KEY_GROUP
            s_b = jnp.dot(k_ref[block_rows(group, b), :], q_heads[h], preferred_element_type=f32)
            key, query = _positions((group * GROUP_BLOCKS + b) * MOBA_BLOCK, MOBA_BLOCK, tile, MOBA_TILE)
            own_start = query_row(tile) // MOBA_BLOCK * MOBA_BLOCK
            masked = jnp.where(picked(group, b, h), s_b, NEG)
            return jnp.where(key >= own_start, jnp.where(key <= query, s_b, NEG), masked)

        return own_score, past_score

    def value(g, b, h):
        return vt_ref[h * HEAD_DIM:(h + 1) * HEAD_DIM, block_rows(g, b)]

    def store_picks(picks):
        for h in range(2):
            for i in range(MOBA_TOPK):
                picks_ref[h * MOBA_TOPK + i:h * MOBA_TOPK + i + 1, :] = picks[h][i]

    q_heads = _split_rows(qt_ref[...])
    pl.when(t == 0)(lambda: store_picks(pick_blocks(t, q_heads)))
    picks = [[picks_ref[h * MOBA_TOPK + i:h * MOBA_TOPK + i + 1, :] for i in range(MOBA_TOPK)] for h in range(2)]
    own_score, past_score = tile_scores(t, q_heads, picks)

    def next_own():
        q_next = _split_rows(qn_ref[...])
        picks_next = pick_blocks(t_next, q_next)
        store_picks(picks_next)
        return tile_scores(t_next, q_next, picks_next)[0]

    _flash_pipeline(2, t == 0, t * MOBA_TILE // KEY_GROUP, own_score, past_score, next_own, value,
                    (_Slot(sa_ref, mxa_ref), _Slot(sb_ref, mxb_ref)), (m_ref, acc_ref))

    heads = [acc_ref[h, :HEAD_DIM, :] * (1.0 / acc_ref[h, HEAD_DIM:HEAD_DIM + 1, :]) for h in range(2)]
    o_ref[...] = jnp.concatenate(heads, axis=0).T.astype(bf16)


def _moba(nat, q_t, v_t, kmean):
    s = nat.shape[0]
    n_blocks = s // MOBA_BLOCK
    last = s // MOBA_TILE - 1
    return pl.pallas_call(
        _moba_kernel,
        grid=(2, s // MOBA_TILE),
        in_specs=[pl.BlockSpec((LANES, MOBA_TILE), lambda p, t: (p, t)),
                  pl.BlockSpec((LANES, MOBA_TILE), lambda p, t: (p, jnp.minimum(t + 1, last))),
                  pl.BlockSpec((s, LANES), lambda p, t: (0, NAT_KA + p)),
                  pl.BlockSpec((LANES, s), lambda p, t: (p, 0)),
                  pl.BlockSpec((n_blocks, LANES), lambda p, t: (0, p))],
        out_specs=pl.BlockSpec((MOBA_TILE, LANES), lambda p, t: (t, p)),
        out_shape=jax.ShapeDtypeStruct((s, 2 * LANES), bf16),
        scratch_shapes=_flash_scratch(2, HEAD_DIM, MOBA_TILE) + [pltpu.VMEM((2 * MOBA_TOPK, MOBA_TILE), f32)],
        compiler_params=_params(2),
        name="moba_attention",
    )(q_t, q_t, nat, v_t, kmean)


def _swa_kernel(sink_ref, q_ref, k_ref, kp_ref, v_ref, vp_ref, o_ref):
    p_id = pl.program_id(0)
    t = pl.program_id(1)
    sub_tiles = SWA_TILE // SWA_WINDOW
    lane = lax.broadcasted_iota(jnp.int32, (1, LANES), 1)
    zero = jnp.zeros((SWA_WINDOW, LANES), bf16)
    qi = lax.broadcasted_iota(jnp.int32, (SWA_WINDOW, 2 * SWA_WINDOW), 0)
    kj = lax.broadcasted_iota(jnp.int32, (SWA_WINDOW, 2 * SWA_WINDOW), 1)
    back = qi + SWA_WINDOW - kj
    in_window = (back >= 0) & (back < SWA_WINDOW)
    for sub in range(sub_tiles):
        rows = slice(sub * SWA_WINDOW, (sub + 1) * SWA_WINDOW)
        if sub == 0:
            k_prev, v_prev = kp_ref[...], vp_ref[...]
            mask = in_window & (kj >= SWA_WINDOW - t * SWA_TILE)
        else:
            prev = slice((sub - 1) * SWA_WINDOW, sub * SWA_WINDOW)
            k_prev, v_prev = k_ref[prev, :], v_ref[prev, :]
            mask = in_window
        k_band = jnp.concatenate([k_prev, k_ref[rows, :]], axis=0)
        v_band = jnp.concatenate([v_prev, v_ref[rows, :]], axis=0)
        q = q_ref[rows, :]
        outs = []
        for h, qh in enumerate((jnp.where(lane < HEAD_DIM, q, zero), jnp.where(lane < HEAD_DIM, zero, q))):
            sink = sink_ref[2 * p_id + h]
            s = lax.dot_general(qh, k_band, _NT, preferred_element_type=f32)
            s = jnp.where(mask, s, NEG)
            m = jnp.maximum(jnp.max(s, axis=-1, keepdims=True), sink)
            p = jnp.exp(s - m)
            denom = jnp.sum(p, axis=-1, keepdims=True) + jnp.exp(sink - m)
            outs.append(jnp.dot(p.astype(bf16), v_band, preferred_element_type=f32) / denom)
        o_ref[rows, :] = jnp.where(lane < HEAD_DIM, outs[0], outs[1]).astype(bf16)


def _swa(nat, sinks):
    s = nat.shape[0]
    ratio = SWA_TILE // SWA_WINDOW

    def prev_rows(t):
        return jnp.maximum(t * ratio - 1, 0)

    return pl.pallas_call(
        _swa_kernel,
        grid=(2, s // SWA_TILE),
        in_specs=[pl.BlockSpec(memory_space=pltpu.SMEM),
                  pl.BlockSpec((SWA_TILE, LANES), lambda p, t: (t, NAT_QB + p)),
                  pl.BlockSpec((SWA_TILE, LANES), lambda p, t: (t, NAT_KB + p)),
                  pl.BlockSpec((SWA_WINDOW, LANES), lambda p, t: (prev_rows(t), NAT_KB + p)),
                  pl.BlockSpec((SWA_TILE, LANES), lambda p, t: (t, NAT_VB + p)),
                  pl.BlockSpec((SWA_WINDOW, LANES), lambda p, t: (prev_rows(t), NAT_VB + p))],
        out_specs=pl.BlockSpec((SWA_TILE, LANES), lambda p, t: (t, p)),
        out_shape=jax.ShapeDtypeStruct((s, 2 * LANES), bf16),
        compiler_params=_params(2),
        name="swa_attention",
    )(sinks, nat, nat, nat, nat, nat)


def _diff_kernel(lam_init, qt_ref, qn_ref, k_ref, vt_ref, lq1_ref, lk1_ref, lq2_ref, lk2_ref, sub_ref,
                 o_ref, m_ref, acc_ref, sa_ref, mxa_ref, sb_ref, mxb_ref):
    t = pl.program_id(1)
    t_next = jnp.minimum(t + 1, pl.num_programs(1) - 1)

    def block_rows(g, b):
        return pl.ds(pl.multiple_of((g * GROUP_BLOCKS + b) * MOBA_BLOCK, MOBA_BLOCK), MOBA_BLOCK)

    def tile_scores(tile, q_ref):
        q_maps = _split_rows(q_ref[...])

        def past_score(g, b, h):
            return jnp.dot(k_ref[block_rows(g, b), :], q_maps[h], preferred_element_type=f32)

        def own_score(b, h):
            group = tile * DIFF_TILE // KEY_GROUP
            key, query = _positions((group * GROUP_BLOCKS + b) * MOBA_BLOCK, MOBA_BLOCK, tile, DIFF_TILE)
            return jnp.where(key <= query, past_score(group, b, h), NEG)

        return own_score, past_score

    def value(g, b, h):
        return vt_ref[:, block_rows(g, b)]

    own_score, past_score = tile_scores(t, qt_ref)
    _flash_pipeline(2, t == 0, t * DIFF_TILE // KEY_GROUP, own_score, past_score,
                    lambda: tile_scores(t_next, qn_ref)[0], value,
                    (_Slot(sa_ref, mxa_ref), _Slot(sb_ref, mxb_ref)), (m_ref, acc_ref))

    lam = (jnp.exp(jnp.sum(lq1_ref[...] * lk1_ref[...], axis=-1, keepdims=True))
           - jnp.exp(jnp.sum(lq2_ref[...] * lk2_ref[...], axis=-1, keepdims=True)) + lam_init)
    maps = [acc_ref[h, :LANES, :] * (1.0 / acc_ref[h, LANES:LANES + 1, :]) for h in range(2)]
    out_t = maps[0] - lam * maps[1]
    inv = lax.rsqrt(jnp.mean(out_t * out_t, axis=0, keepdims=True) + EPS) * (1.0 - lam_init)
    o_ref[...] = ((out_t * inv).T * sub_ref[...]).astype(bf16)


def _diff(nat, q_t, v_t, lq1, lk1, lq2, lk2, subln, lam_init):
    s = nat.shape[0]
    vec = _const_spec((1, HEAD_DIM))
    last = s // DIFF_TILE - 1
    return pl.pallas_call(
        functools.partial(_diff_kernel, lam_init),
        grid=(4, s // DIFF_TILE),
        in_specs=[pl.BlockSpec((LANES, DIFF_TILE), lambda h, t: (2 + h, t)),
                  pl.BlockSpec((LANES, DIFF_TILE), lambda h, t: (2 + h, jnp.minimum(t + 1, last))),
                  pl.BlockSpec((s, LANES), lambda h, t: (0, NAT_KC + h)),
                  pl.BlockSpec((LANES, s), lambda h, t: (2 + h, 0)),
                  vec, vec, vec, vec, _const_spec((1, LANES))],
        out_specs=pl.BlockSpec((DIFF_TILE, LANES), lambda h, t: (t, h)),
        out_shape=jax.ShapeDtypeStruct((s, 4 * LANES), bf16),
        scratch_shapes=_flash_scratch(2, LANES, DIFF_TILE),
        compiler_params=_params(2),
        name="diff_attention",
    )(q_t, q_t, nat, v_t, lq1, lk1, lq2, lk2, subln)


def _merge_kernel(x_ref, g_ref, wgate_ref, oa_ref, ob_ref, oc_ref, pa_ref, pb_ref, pc_ref, wo_ref, o_ref):
    x = x_ref[...]
    h = _rms(x, g_ref[...]).astype(bf16)
    branches = ((oa_ref, pa_ref), (ob_ref, pb_ref), (oc_ref, pc_ref))
    merged = jnp.zeros_like(x)
    for i, (b_ref, p_ref) in enumerate(branches):
        logits = jnp.dot(h, wgate_ref[:, i * D_MODEL:(i + 1) * D_MODEL], preferred_element_type=f32)
        y = jnp.dot(b_ref[...], p_ref[...], preferred_element_type=f32)
        merged = merged + jax.nn.sigmoid(logits) * y
    o_ref[...] = x + jnp.dot(merged.astype(bf16), wo_ref[...], preferred_element_type=f32)


def _merge(x, gain, w_gate, oa, ob, oc, pa, pb, pc, w_out, layer):
    s = x.shape[0]

    def rows(width):
        return pl.BlockSpec((ROW_TILE, width), lambda i: (i, 0))

    return pl.pallas_call(
        _merge_kernel,
        grid=(s // ROW_TILE,),
        in_specs=[rows(D_MODEL), _const_spec((1, D_MODEL)), _const_spec((D_MODEL, GATE_COLS)),
                  rows(oa.shape[1]), rows(ob.shape[1]), rows(oc.shape[1]),
                  _layer_spec(pa.shape[1:], layer), _layer_spec(pb.shape[1:], layer),
                  _layer_spec(pc.shape[1:], layer), _layer_spec((D_MODEL, D_MODEL), layer)],
        out_specs=rows(D_MODEL),
        out_shape=jax.ShapeDtypeStruct(x.shape, f32),
        compiler_params=_params(1),
        name="gated_merge",
    )(x, gain, w_gate, oa, ob, oc, pa, pb, pc, w_out)


def _rope_tables(seq):
    pos = jnp.arange(seq, dtype=f32)
    inv_freq = ROPE_THETA ** (-jnp.arange(0, HEAD_DIM, 2, dtype=f32) / HEAD_DIM)
    ang = pos[:, None] * inv_freq[None, :]
    cos, sin = jnp.cos(ang), jnp.sin(ang)
    return (jnp.tile(cos, (1, 4)), jnp.concatenate([-sin, sin, -sin, sin], axis=-1), cos.T, sin.T)


def _layer_weights(w_in):
    qa, ka, va = w_in[:, 0:256], w_in[:, 256:512], w_in[:, 512:768]
    qb, kb, vb = w_in[:, 768:1024], w_in[:, 1024:1152], w_in[:, 1152:1280]
    qc, kc, vc = w_in[:, 1280:1792], w_in[:, 1792:2304], w_in[:, 2304:2816]
    gates = w_in[:, 2816:]

    def dup(w):
        h0, h1 = w[:, :HEAD_DIM], w[:, HEAD_DIM:]
        return jnp.concatenate([h0, h0, h1, h1], axis=1)

    w_nat = jnp.concatenate([ka, kc, qb, dup(kb), dup(vb)], axis=1)
    w_t = jnp.concatenate([qa, qc, va, vc], axis=1).T
    return w_nat, w_t, gates


def _head_gains(qa, ka, qb, kb, qc, kc):
    scale = HEAD_DIM ** -0.5
    scale2 = scale * math.log2(math.e)
    nat = jnp.concatenate([jnp.tile(ka, 4), jnp.tile(kc, 8), jnp.tile(qb * scale, 4), jnp.tile(kb, 4)])
    feat = jnp.concatenate([jnp.tile(qa * scale2, 4), jnp.tile(qc * scale2, 8)])
    return nat[None, :], feat[:, None]


def kernel(x, ffn1_norm, ffn1_w_gate, ffn1_w_up, ffn1_w_down, mix_norm, w_in, moba_q_norm, moba_k_norm, swa_q_norm, swa_k_norm, swa_sinks, diff_q_norm, diff_k_norm, diff_lambda_q1, diff_lambda_k1, diff_lambda_q2, diff_lambda_k2, diff_subln, w_branch_a, w_branch_b, w_branch_c, w_out, ffn2_norm, ffn2_w_gate, ffn2_w_up, ffn2_w_down):
    batch, seq, _ = x.shape
    assert batch == 1 and seq % SWA_TILE == 0 and seq % ROW_TILE == 0 and seq % KEY_GROUP == 0
    depth = w_in.shape[0]
    cos, sin, cos_t, sin_t = _rope_tables(seq)
    ffn1 = [w.astype(bf16) for w in (ffn1_w_gate, ffn1_w_up, ffn1_w_down)]
    ffn2 = [w.astype(bf16) for w in (ffn2_w_gate, ffn2_w_up, ffn2_w_down)]
    out_w = [w.astype(bf16) for w in (w_branch_a, w_branch_b, w_branch_c, w_out)]
    w_in = w_in.astype(bf16)
    xs = x[0]
    for l in range(depth):
        lam_init = 0.8 - 0.6 * math.exp(-0.3 * l)
        xs = _ffn(xs, ffn1_norm[l][None], *ffn1, l)
        w_nat, w_t, w_gate = _layer_weights(w_in[l])
        hg_nat, hg_t = _head_gains(moba_q_norm[l], moba_k_norm[l], swa_q_norm[l], swa_k_norm[l],
                                   diff_q_norm[l], diff_k_norm[l])
        nat, q_t, v_t, kmean = _proj(xs, mix_norm[l][None], w_nat, w_t, hg_nat, hg_t, cos, sin, cos_t, sin_t)
        oa = _moba(nat, q_t, v_t, kmean.reshape(seq // MOBA_BLOCK, 2 * LANES))
        ob = _swa(nat, swa_sinks[l])
        oc = _diff(nat, q_t, v_t, diff_lambda_q1[l][None], diff_lambda_k1[l][None], diff_lambda_q2[l][None],
                   diff_lambda_k2[l][None], diff_subln[l][None], lam_init)
        xs = _merge(xs, mix_norm[l][None], w_gate, oa, ob, oc, *out_w, l)
        xs = _ffn(xs, ffn2_norm[l][None], *ffn2, l)
    return xs[None]
```

```python
import functools
import math

import jax
import jax.numpy as jnp
from jax import lax
from jax.experimental import pallas as pl
from jax.experimental.pallas import tpu as pltpu

D_MODEL = 1024
D_FF = 2816
HEAD_DIM = 64
HALF = HEAD_DIM // 2
LANES = 128
MOBA_BLOCK = 256
MOBA_TOPK = 3
SWA_WINDOW = 128
N_BRANCH = 3
ROPE_THETA = 10000.0
EPS = 1e-6
NEG = -1e30
FFN_HALF = 0.5

NAT_QK_COLS = 1280
NAT_COLS = 1536
NAT_KA, NAT_KC, NAT_QB, NAT_KB, NAT_VB = 0, 2, 6, 8, 10
QT_ROWS = 768
VT_ROWS = 768
GATE_COLS = N_BRANCH * D_MODEL

FFN_TILE = 512
ROW_TILE = 512
MERGE_TILE = 1024
MOBA_TILE = 512
DIFF_TILE = 512
GROUP_BLOCKS = 4
KEY_GROUP = GROUP_BLOCKS * MOBA_BLOCK
STAGES_PER_TRIP = 4
PARK_LEAD = 1
SUM_ROWS = 16
SWA_TILE = 4096
VMEM_LIMIT = 48 * 1024 * 1024

f32 = jnp.float32
bf16 = jnp.bfloat16

_NT = (((1,), (1,)), ((), ()))


def _rms(x, gain):
    return x * lax.rsqrt(jnp.mean(x * x, axis=-1, keepdims=True) + EPS) * gain


def _params(n_axes):
    return pltpu.CompilerParams(
        dimension_semantics=("arbitrary",) * n_axes, vmem_limit_bytes=VMEM_LIMIT)


def _const_spec(shape):
    return pl.BlockSpec(shape, lambda *_: (0,) * len(shape), pipeline_mode=pl.Buffered(1))


def _ffn_kernel(x_ref, g_ref, wg_ref, wu_ref, wd_ref, o_ref):
    x = x_ref[...]
    h = _rms(x, g_ref[...]).astype(bf16)
    a = jnp.dot(h, wg_ref[...], preferred_element_type=f32)
    b = jnp.dot(h, wu_ref[...], preferred_element_type=f32)
    act = (a * jax.nn.sigmoid(a) * b).astype(bf16)
    o_ref[...] = x + FFN_HALF * jnp.dot(act, wd_ref[...], preferred_element_type=f32)


def _layer_spec(shape, layer):
    return pl.BlockSpec((None,) + tuple(shape), lambda *_: (layer,) + (0,) * len(shape),
                        pipeline_mode=pl.Buffered(1))


def _ffn(x, gain, wg, wu, wd, layer):
    s = x.shape[0]
    row = pl.BlockSpec((FFN_TILE, D_MODEL), lambda i: (i, 0))
    return pl.pallas_call(
        _ffn_kernel,
        grid=(s // FFN_TILE,),
        in_specs=[row, _const_spec((1, D_MODEL)), _layer_spec((D_MODEL, D_FF), layer),
                  _layer_spec((D_MODEL, D_FF), layer), _layer_spec((D_FF, D_MODEL), layer)],
        out_specs=row,
        out_shape=jax.ShapeDtypeStruct(x.shape, f32),
        compiler_params=_params(1),
        name="ffn",
    )(x, gain, wg, wu, wd)


def _proj_kernel(x_ref, g_ref, wn_ref, wt_ref, hg_ref, hgt_ref, cos_ref, sin_ref, cost_ref, sint_ref,
                 nat_ref, qt_ref, vt_ref, km_ref):
    x = x_ref[...]
    h = _rms(x, g_ref[...]).astype(bf16)

    proj = jnp.dot(h, wn_ref[...], preferred_element_type=f32)
    lane = lax.broadcasted_iota(jnp.int32, (1, LANES), 1)
    first_head = lane < HEAD_DIM
    first_half = (lane % HEAD_DIM) < HALF
    cos = cos_ref[...]
    sin = sin_ref[...]
    for c in range(NAT_QK_COLS // LANES):
        cols = slice(c * LANES, (c + 1) * LANES)
        y = proj[:, cols]
        sq = y * y
        ss0 = jnp.sum(jnp.where(first_head, sq, 0.0), axis=-1, keepdims=True)
        ss1 = jnp.sum(jnp.where(first_head, 0.0, sq), axis=-1, keepdims=True)
        inv = jnp.where(first_head, lax.rsqrt(ss0 / HEAD_DIM + EPS), lax.rsqrt(ss1 / HEAD_DIM + EPS))
        y = y * inv * hg_ref[:, cols]
        partner = jnp.where(first_half, pltpu.roll(y, LANES - HALF, 1), pltpu.roll(y, HALF, 1))
        y = y * cos + partner * sin
        nat_ref[:, cols] = y.astype(bf16)
        if c in (NAT_KA, NAT_KA + 1):
            for b in range(ROW_TILE // MOBA_BLOCK):
                blk = y[b * MOBA_BLOCK:(b + 1) * MOBA_BLOCK]
                km_ref[b, :, (c - NAT_KA) * LANES:(c - NAT_KA + 1) * LANES] = jnp.mean(blk, axis=0, keepdims=True)
    nat_ref[:, NAT_QK_COLS:] = proj[:, NAT_QK_COLS:].astype(bf16)

    proj_t = lax.dot_general(wt_ref[...], h, _NT, preferred_element_type=f32)
    cos_t = cost_ref[...]
    sin_t = sint_ref[...]
    for c in range(QT_ROWS // HEAD_DIM):
        rows = slice(c * HEAD_DIM, (c + 1) * HEAD_DIM)
        y = proj_t[rows, :]
        inv = lax.rsqrt(jnp.mean(y * y, axis=0, keepdims=True) + EPS)
        y = y * inv * hgt_ref[rows, :]
        y1, y2 = y[:HALF], y[HALF:]
        qt_ref[c * HEAD_DIM:c * HEAD_DIM + HALF, :] = (y1 * cos_t - y2 * sin_t).astype(bf16)
        qt_ref[c * HEAD_DIM + HALF:(c + 1) * HEAD_DIM, :] = (y2 * cos_t + y1 * sin_t).astype(bf16)
    vt_ref[...] = proj_t[QT_ROWS:, :].astype(bf16)


def _proj(x, gain, w_nat, w_t, hg_nat, hg_t, cos, sin, cos_t, sin_t):
    s = x.shape[0]
    nb = ROW_TILE // MOBA_BLOCK
    return pl.pallas_call(
        _proj_kernel,
        grid=(s // ROW_TILE,),
        in_specs=[pl.BlockSpec((ROW_TILE, D_MODEL), lambda i: (i, 0)),
                  _const_spec((1, D_MODEL)),
                  _const_spec((D_MODEL, NAT_COLS)),
                  _const_spec((QT_ROWS + VT_ROWS, D_MODEL)),
                  _const_spec((1, NAT_QK_COLS)),
                  _const_spec((QT_ROWS, 1)),
                  pl.BlockSpec((ROW_TILE, LANES), lambda i: (i, 0)),
                  pl.BlockSpec((ROW_TILE, LANES), lambda i: (i, 0)),
                  pl.BlockSpec((HALF, ROW_TILE), lambda i: (0, i)),
                  pl.BlockSpec((HALF, ROW_TILE), lambda i: (0, i))],
        out_specs=[pl.BlockSpec((ROW_TILE, NAT_COLS), lambda i: (i, 0)),
                   pl.BlockSpec((QT_ROWS, ROW_TILE), lambda i: (0, i)),
                   pl.BlockSpec((VT_ROWS, ROW_TILE), lambda i: (0, i)),
                   pl.BlockSpec((nb, 1, 2 * LANES), lambda i: (i, 0, 0))],
        out_shape=[jax.ShapeDtypeStruct((s, NAT_COLS), bf16),
                   jax.ShapeDtypeStruct((QT_ROWS, s), bf16),
                   jax.ShapeDtypeStruct((VT_ROWS, s), bf16),
                   jax.ShapeDtypeStruct((s // MOBA_BLOCK, 1, 2 * LANES), f32)],
        compiler_params=_params(1),
        name="qkv_proj",
    )(x, gain, w_nat, w_t, hg_nat, hg_t, cos, sin, cos_t, sin_t)


def _split_rows(q_t):
    zero = jnp.zeros((HEAD_DIM, q_t.shape[1]), q_t.dtype)
    return (jnp.concatenate([q_t[:HEAD_DIM], zero], axis=0),
            jnp.concatenate([zero, q_t[HEAD_DIM:]], axis=0))


class _Slot:
    def __init__(self, s_ref, mx_ref):
        self.s, self.mx = s_ref, mx_ref


def _with_ones(v_t):
    return jnp.concatenate([v_t, jnp.ones((SUM_ROWS, v_t.shape[1]), v_t.dtype)], axis=0)


def _flash_stage(n_maps, state, score=None, nxt=None, cur=None, value=None):
    m_ref, acc_ref = state
    maps = range(n_maps)
    if cur is not None:
        m_old = [m_ref[h] for h in maps]
        m_new = [jnp.maximum(m_old[h], cur.mx[h]) for h in maps]
        for h in maps:
            acc_ref[h] = jnp.exp2(m_old[h] - m_new[h]) * acc_ref[h]
            m_ref[h] = m_new[h]
    col_max = [None] * n_maps

    def park(b):
        rows = slice(b * MOBA_BLOCK, (b + 1) * MOBA_BLOCK)
        for h in maps:
            s_b = score(b, h)
            nxt.s[h, rows, :] = s_b
            top = jnp.max(s_b, axis=0, keepdims=True)
            col_max[h] = top if b == 0 else jnp.maximum(col_max[h], top)
        if b == GROUP_BLOCKS - 1:
            for h in maps:
                nxt.mx[h] = col_max[h]

    def consume(b):
        rows = slice(b * MOBA_BLOCK, (b + 1) * MOBA_BLOCK)
        for h in maps:
            p = jnp.exp2(cur.s[h, rows, :] - m_new[h]).astype(bf16)
            acc_ref[h] += jnp.dot(_with_ones(value(b, h)), p, preferred_element_type=f32)

    if cur is not None and nxt is cur:
        for b in range(GROUP_BLOCKS):
            consume(b)
            park(b)
        return
    for step in range(GROUP_BLOCKS + PARK_LEAD):
        if score is not None and step < GROUP_BLOCKS:
            park(step)
        if cur is not None and step >= PARK_LEAD:
            consume(step - PARK_LEAD)


def _flash_pipeline(n_maps, first_tile, own_group, own_score, past_score, next_own, value, slots, state):
    m_ref, acc_ref = state
    m_ref[...] = jnp.full(m_ref.shape, NEG, f32)
    acc_ref[...] = jnp.zeros(acc_ref.shape, f32)

    def stage(base, offset, last):
        i = base + offset
        cur = slots[offset % 2]
        nxt = slots[0] if last else slots[1 - offset % 2]
        group = jnp.where(i == 0, own_group, i - 1)
        score = next_own() if last else (lambda b, h: past_score(i, b, h))
        _flash_stage(n_maps, state, score=score, nxt=nxt, cur=cur, value=lambda b, h: value(group, b, h))

    pl.when(first_tile)(functools.partial(_flash_stage, n_maps, state, score=own_score, nxt=slots[0]))

    def trip(k, carry):
        for offset in range(STAGES_PER_TRIP):
            stage(k * STAGES_PER_TRIP, offset, False)
        return carry

    full_trips = own_group // STAGES_PER_TRIP
    lax.fori_loop(0, full_trips, trip, 0)

    base = full_trips * STAGES_PER_TRIP
    left = own_group - base
    for offset in range(STAGES_PER_TRIP):
        if offset > 0:
            pl.when(left >= offset)(functools.partial(stage, base, offset - 1, False))
        pl.when(left == offset)(functools.partial(stage, base, offset, True))


def _flash_scratch(n_maps, dv, tq):
    row = pltpu.VMEM((n_maps, 1, tq), f32)
    scores = pltpu.VMEM((n_maps, KEY_GROUP, tq), f32)
    return [row, pltpu.VMEM((n_maps, dv + SUM_ROWS, tq), f32), scores, row, scores, row]


def _positions(first_key, n_keys, t, tq):
    key = first_key + lax.broadcasted_iota(jnp.int32, (n_keys, tq), 0)
    query = t * tq + lax.broadcasted_iota(jnp.int32, (n_keys, tq), 1)
    return key, query


def _moba_kernel(qt_ref, qn_ref, k_ref, vt_ref, km_ref, o_ref, m_ref, acc_ref, sa_ref, mxa_ref, sb_ref, mxb_ref,
                 picks_ref):
    t = pl.program_id(1)
    t_next = jnp.minimum(t + 1, pl.num_programs(1) - 1)
    n_blocks = km_ref.shape[0]

    km = km_ref[...]
    km_hi = km.astype(bf16)
    rem = km - km_hi.astype(f32)
    km_mid = rem.astype(bf16)
    km_lo = (rem - km_mid.astype(f32)).astype(bf16)
    blk = lax.broadcasted_iota(jnp.int32, (n_blocks, 1), 0).astype(f32)

    def block_rows(g, b):
        return pl.ds(pl.multiple_of((g * GROUP_BLOCKS + b) * MOBA_BLOCK, MOBA_BLOCK), MOBA_BLOCK)

    def query_row(tile):
        return tile * MOBA_TILE + lax.broadcasted_iota(jnp.int32, (1, MOBA_TILE), 1)

    def pick_blocks(tile, q_heads):
        own_blk = (query_row(tile) // MOBA_BLOCK).astype(f32)
        picks = []
        for qh in q_heads:
            gate = (jnp.dot(km_hi, qh, preferred_element_type=f32)
                    + jnp.dot(km_mid, qh, preferred_element_type=f32)
                    + jnp.dot(km_lo, qh, preferred_element_type=f32))
            gate = jnp.where(blk < own_blk, gate, NEG)
            chosen = []
            for _ in range(MOBA_TOPK):
                best = jnp.max(gate, axis=0, keepdims=True)
                idx = jnp.min(jnp.where(gate == best, blk, 1e9), axis=0, keepdims=True)
                chosen.append(jnp.where(best > 0.5 * NEG, idx, -1.0))
                gate = jnp.where(blk == idx, 2.0 * NEG, gate)
            picks.append(chosen)
        return picks

    def tile_scores(tile, q_heads, picks):
        def picked(g, b, h):
            blk_f = (g * GROUP_BLOCKS + b).astype(f32)
            return (picks[h][0] == blk_f) | (picks[h][1] == blk_f) | (picks[h][2] == blk_f)

        def past_score(g, b, h):
            bias = jnp.where(picked(g, b, h), 0.0, NEG)
            first_row = lax.broadcasted_iota(jnp.int32, (SUM_ROWS, MOBA_TILE), 0) == 0
            bias_rows = jnp.where(first_row, bias, 0.0).astype(bf16)
            rest = jnp.zeros((LANES - SUM_ROWS, MOBA_TILE), bf16)
            rhs = jnp.concatenate([q_heads[h], bias_rows, rest], axis=0)
            lhs = jnp.concatenate([k_ref[block_rows(g, b), :], jnp.ones((MOBA_BLOCK, LANES), bf16)], axis=1)
            return jnp.dot(lhs, rhs, preferred_element_type=f32)

        def own_score(b, h):
            group = tile * MOBA_TILE // KEY_GROUP
            s_b = jnp.dot(k_ref[block_rows(group, b), :], q_heads[h], preferred_element_type=f32)
            key, query = _positions((group * GROUP_BLOCKS + b) * MOBA_BLOCK, MOBA_BLOCK, tile, MOBA_TILE)
            own_start = query_row(tile) // MOBA_BLOCK * MOBA_BLOCK
            masked = jnp.where(picked(group, b, h), s_b, NEG)
            return jnp.where(key >= own_start, jnp.where(key <= query, s_b, NEG), masked)

        return own_score, past_score

    def value(g, b, h):
        return vt_ref[h * HEAD_DIM:(h + 1) * HEAD_DIM, block_rows(g, b)]

    def store_picks(picks):
        for h in range(2):
            for i in range(MOBA_TOPK):
                picks_ref[h * MOBA_TOPK + i:h * MOBA_TOPK + i + 1, :] = picks[h][i]

    q_heads = _split_rows(qt_ref[...])
    pl.when(t == 0)(lambda: store_picks(pick_blocks(t, q_heads)))
    picks = [[picks_ref[h * MOBA_TOPK + i:h * MOBA_TOPK + i + 1, :] for i in range(MOBA_TOPK)] for h in range(2)]
    own_score, past_score = tile_scores(t, q_heads, picks)

    def next_own():
        q_next = _split_rows(qn_ref[...])
        picks_next = pick_blocks(t_next, q_next)
        store_picks(picks_next)
        return tile_scores(t_next, q_next, picks_next)[0]

    _flash_pipeline(2, t == 0, t * MOBA_TILE // KEY_GROUP, own_score, past_score, next_own, value,
                    (_Slot(sa_ref, mxa_ref), _Slot(sb_ref, mxb_ref)), (m_ref, acc_ref))

    heads = [acc_ref[h, :HEAD_DIM, :] * (1.0 / acc_ref[h, HEAD_DIM:HEAD_DIM + 1, :]) for h in range(2)]
    o_ref[...] = jnp.concatenate(heads, axis=0).T.astype(bf16)


def _moba(nat, q_t, v_t, kmean):
    s = nat.shape[0]
    n_blocks = s // MOBA_BLOCK
    last = s // MOBA_TILE - 1
    return pl.pallas_call(
        _moba_kernel,
        grid=(2, s // MOBA_TILE),
        in_specs=[pl.BlockSpec((LANES, MOBA_TILE), lambda p, t: (p, t)),
                  pl.BlockSpec((LANES, MOBA_TILE), lambda p, t: (p, jnp.minimum(t + 1, last))),
                  pl.BlockSpec((s, LANES), lambda p, t: (0, NAT_KA + p)),
                  pl.BlockSpec((LANES, s), lambda p, t: (p, 0)),
                  pl.BlockSpec((n_blocks, LANES), lambda p, t: (0, p))],
        out_specs=pl.BlockSpec((MOBA_TILE, LANES), lambda p, t: (t, p)),
        out_shape=jax.ShapeDtypeStruct((s, 2 * LANES), bf16),
        scratch_shapes=_flash_scratch(2, HEAD_DIM, MOBA_TILE) + [pltpu.VMEM((2 * MOBA_TOPK, MOBA_TILE), f32)],
        compiler_params=_params(2),
        name="moba_attention",
    )(q_t, q_t, nat, v_t, kmean)


def _swa_kernel(sink_ref, q_ref, k_ref, kp_ref, v_ref, vp_ref, o_ref):
    p_id = pl.program_id(0)
    t = pl.program_id(1)
    sub_tiles = SWA_TILE // SWA_WINDOW
    lane = lax.broadcasted_iota(jnp.int32, (1, LANES), 1)
    zero = jnp.zeros((SWA_WINDOW, LANES), bf16)
    qi = lax.broadcasted_iota(jnp.int32, (SWA_WINDOW, 2 * SWA_WINDOW), 0)
    kj = lax.broadcasted_iota(jnp.int32, (SWA_WINDOW, 2 * SWA_WINDOW), 1)
    back = qi + SWA_WINDOW - kj
    in_window = (back >= 0) & (back < SWA_WINDOW)
    for sub in range(sub_tiles):
        rows = slice(sub * SWA_WINDOW, (sub + 1) * SWA_WINDOW)
        if sub == 0:
            k_prev, v_prev = kp_ref[...], vp_ref[...]
            mask = in_window & (kj >= SWA_WINDOW - t * SWA_TILE)
        else:
            prev = slice((sub - 1) * SWA_WINDOW, sub * SWA_WINDOW)
            k_prev, v_prev = k_ref[prev, :], v_ref[prev, :]
            mask = in_window
        k_band = jnp.concatenate([k_prev, k_ref[rows, :]], axis=0)
        v_band = jnp.concatenate([v_prev, v_ref[rows, :]], axis=0)
        q = q_ref[rows, :]
        outs = []
        for h, qh in enumerate((jnp.where(lane < HEAD_DIM, q, zero), jnp.where(lane < HEAD_DIM, zero, q))):
            sink = sink_ref[2 * p_id + h]
            s = lax.dot_general(qh, k_band, _NT, preferred_element_type=f32)
            s = jnp.where(mask, s, NEG)
            m = jnp.maximum(jnp.max(s, axis=-1, keepdims=True), sink)
            p = jnp.exp(s - m)
            denom = jnp.sum(p, axis=-1, keepdims=True) + jnp.exp(sink - m)
            outs.append(jnp.dot(p.astype(bf16), v_band, preferred_element_type=f32) / denom)
        o_ref[rows, :] = jnp.where(lane < HEAD_DIM, outs[0], outs[1]).astype(bf16)


def _swa(nat, sinks):
    s = nat.shape[0]
    ratio = SWA_TILE // SWA_WINDOW

    def prev_rows(t):
        return jnp.maximum(t * ratio - 1, 0)

    return pl.pallas_call(
        _swa_kernel,
        grid=(2, s // SWA_TILE),
        in_specs=[pl.BlockSpec(memory_space=pltpu.SMEM),
                  pl.BlockSpec((SWA_TILE, LANES), lambda p, t: (t, NAT_QB + p)),
                  pl.BlockSpec((SWA_TILE, LANES), lambda p, t: (t, NAT_KB + p)),
                  pl.BlockSpec((SWA_WINDOW, LANES), lambda p, t: (prev_rows(t), NAT_KB + p)),
                  pl.BlockSpec((SWA_TILE, LANES), lambda p, t: (t, NAT_VB + p)),
                  pl.BlockSpec((SWA_WINDOW, LANES), lambda p, t: (prev_rows(t), NAT_VB + p))],
        out_specs=pl.BlockSpec((SWA_TILE, LANES), lambda p, t: (t, p)),
        out_shape=jax.ShapeDtypeStruct((s, 2 * LANES), bf16),
        compiler_params=_params(2),
        name="swa_attention",
    )(sinks, nat, nat, nat, nat, nat)


def _diff_kernel(lam_init, qt_ref, qn_ref, k_ref, vt_ref, lq1_ref, lk1_ref, lq2_ref, lk2_ref, sub_ref,
                 o_ref, m_ref, acc_ref, sa_ref, mxa_ref, sb_ref, mxb_ref):
    t = pl.program_id(1)
    t_next = jnp.minimum(t + 1, pl.num_programs(1) - 1)

    def block_rows(g, b):
        return pl.ds(pl.multiple_of((g * GROUP_BLOCKS + b) * MOBA_BLOCK, MOBA_BLOCK), MOBA_BLOCK)

    def tile_scores(tile, q_ref):
        q_maps = _split_rows(q_ref[...])

        def past_score(g, b, h):
            return jnp.dot(k_ref[block_rows(g, b), :], q_maps[h], preferred_element_type=f32)

        def own_score(b, h):
            group = tile * DIFF_TILE // KEY_GROUP
            key, query = _positions((group * GROUP_BLOCKS + b) * MOBA_BLOCK, MOBA_BLOCK, tile, DIFF_TILE)
            return jnp.where(key <= query, past_score(group, b, h), NEG)

        return own_score, past_score

    def value(g, b, h):
        return vt_ref[:, block_rows(g, b)]

    own_score, past_score = tile_scores(t, qt_ref)
    _flash_pipeline(2, t == 0, t * DIFF_TILE // KEY_GROUP, own_score, past_score,
                    lambda: tile_scores(t_next, qn_ref)[0], value,
                    (_Slot(sa_ref, mxa_ref), _Slot(sb_ref, mxb_ref)), (m_ref, acc_ref))

    lam = (jnp.exp(jnp.sum(lq1_ref[...] * lk1_ref[...], axis=-1, keepdims=True))
           - jnp.exp(jnp.sum(lq2_ref[...] * lk2_ref[...], axis=-1, keepdims=True)) + lam_init)
    maps = [acc_ref[h, :LANES, :] * (1.0 / acc_ref[h, LANES:LANES + 1, :]) for h in range(2)]
    out_t = maps[0] - lam * maps[1]
    inv = lax.rsqrt(jnp.mean(out_t * out_t, axis=0, keepdims=True) + EPS) * (1.0 - lam_init)
    o_ref[...] = ((out_t * inv).T * sub_ref[...]).astype(bf16)


def _diff(nat, q_t, v_t, lq1, lk1, lq2, lk2, subln, lam_init):
    s = nat.shape[0]
    vec = _const_spec((1, HEAD_DIM))
    last = s // DIFF_TILE - 1
    return pl.pallas_call(
        functools.partial(_diff_kernel, lam_init),
        grid=(4, s // DIFF_TILE),
        in_specs=[pl.BlockSpec((LANES, DIFF_TILE), lambda h, t: (2 + h, t)),
                  pl.BlockSpec((LANES, DIFF_TILE), lambda h, t: (2 + h, jnp.minimum(t + 1, last))),
                  pl.BlockSpec((s, LANES), lambda h, t: (0, NAT_KC + h)),
                  pl.BlockSpec((LANES, s), lambda h, t: (2 + h, 0)),
                  vec, vec, vec, vec, _const_spec((1, LANES))],
        out_specs=pl.BlockSpec((DIFF_TILE, LANES), lambda h, t: (t, h)),
        out_shape=jax.ShapeDtypeStruct((s, 4 * LANES), bf16),
        scratch_shapes=_flash_scratch(2, LANES, DIFF_TILE),
        compiler_params=_params(2),
        name="diff_attention",
    )(q_t, q_t, nat, v_t, lq1, lk1, lq2, lk2, subln)


def _merge_kernel(x_ref, g_ref, wgate_ref, oa_ref, ob_ref, oc_ref, pa_ref, pb_ref, pc_ref, wo_ref, o_ref):
    x = x_ref[...]
    h = _rms(x, g_ref[...]).astype(bf16)
    branches = ((oa_ref, pa_ref), (ob_ref, pb_ref), (oc_ref, pc_ref))
    merged = jnp.zeros_like(x)
    for i, (b_ref, p_ref) in enumerate(branches):
        logits = jnp.dot(h, wgate_ref[:, i * D_MODEL:(i + 1) * D_MODEL], preferred_element_type=f32)
        y = jnp.dot(b_ref[...], p_ref[...], preferred_element_type=f32)
        merged = merged + jax.nn.sigmoid(logits) * y
    o_ref[...] = x + jnp.dot(merged.astype(bf16), wo_ref[...], preferred_element_type=f32)


def _merge(x, gain, w_gate, oa, ob, oc, pa, pb, pc, w_out, layer):
    s = x.shape[0]

    def rows(width):
        return pl.BlockSpec((MERGE_TILE, width), lambda i: (i, 0))

    return pl.pallas_call(
        _merge_kernel,
        grid=(s // MERGE_TILE,),
        in_specs=[rows(D_MODEL), _const_spec((1, D_MODEL)), _const_spec((D_MODEL, GATE_COLS)),
                  rows(oa.shape[1]), rows(ob.shape[1]), rows(oc.shape[1]),
                  _layer_spec(pa.shape[1:], layer), _layer_spec(pb.shape[1:], layer),
                  _layer_spec(pc.shape[1:], layer), _layer_spec((D_MODEL, D_MODEL), layer)],
        out_specs=rows(D_MODEL),
        out_shape=jax.ShapeDtypeStruct(x.shape, f32),
        compiler_params=_params(1),
        name="gated_merge",
    )(x, gain, w_gate, oa, ob, oc, pa, pb, pc, w_out)


def _rope_tables(seq):
    pos = jnp.arange(seq, dtype=f32)
    inv_freq = ROPE_THETA ** (-jnp.arange(0, HEAD_DIM, 2, dtype=f32) / HEAD_DIM)
    ang = pos[:, None] * inv_freq[None, :]
    cos, sin = jnp.cos(ang), jnp.sin(ang)
    return (jnp.tile(cos, (1, 4)), jnp.concatenate([-sin, sin, -sin, sin], axis=-1), cos.T, sin.T)


def _layer_weights(w_in):
    qa, ka, va = w_in[:, 0:256], w_in[:, 256:512], w_in[:, 512:768]
    qb, kb, vb = w_in[:, 768:1024], w_in[:, 1024:1152], w_in[:, 1152:1280]
    qc, kc, vc = w_in[:, 1280:1792], w_in[:, 1792:2304], w_in[:, 2304:2816]
    gates = w_in[:, 2816:]

    def dup(w):
        h0, h1 = w[:, :HEAD_DIM], w[:, HEAD_DIM:]
        return jnp.concatenate([h0, h0, h1, h1], axis=1)

    w_nat = jnp.concatenate([ka, kc, qb, dup(kb), dup(vb)], axis=1)
    w_t = jnp.concatenate([qa, qc, va, vc], axis=1).T
    return w_nat, w_t, gates


def _head_gains(qa, ka, qb, kb, qc, kc):
    scale = HEAD_DIM ** -0.5
    scale2 = scale * math.log2(math.e)
    nat = jnp.concatenate([jnp.tile(ka, 4), jnp.tile(kc, 8), jnp.tile(qb * scale, 4), jnp.tile(kb, 4)])
    feat = jnp.concatenate([jnp.tile(qa * scale2, 4), jnp.tile(qc * scale2, 8)])
    return nat[None, :], feat[:, None]


def kernel(x, ffn1_norm, ffn1_w_gate, ffn1_w_up, ffn1_w_down, mix_norm, w_in, moba_q_norm, moba_k_norm, swa_q_norm, swa_k_norm, swa_sinks, diff_q_norm, diff_k_norm, diff_lambda_q1, diff_lambda_k1, diff_lambda_q2, diff_lambda_k2, diff_subln, w_branch_a, w_branch_b, w_branch_c, w_out, ffn2_norm, ffn2_w_gate, ffn2_w_up, ffn2_w_down):
    batch, seq, _ = x.shape
    assert batch == 1 and seq % SWA_TILE == 0 and seq % ROW_TILE == 0 and seq % KEY_GROUP == 0
    depth = w_in.shape[0]
    cos, sin, cos_t, sin_t = _rope_tables(seq)
    ffn1 = [w.astype(bf16) for w in (ffn1_w_gate, ffn1_w_up, ffn1_w_down)]
    ffn2 = [w.astype(bf16) for w in (ffn2_w_gate, ffn2_w_up, ffn2_w_down)]
    out_w = [w.astype(bf16) for w in (w_branch_a, w_branch_b, w_branch_c, w_out)]
    w_in = w_in.astype(bf16)
    xs = x[0]
    for l in range(depth):
        lam_init = 0.8 - 0.6 * math.exp(-0.3 * l)
        xs = _ffn(xs, ffn1_norm[l][None], *ffn1, l)
        w_nat, w_t, w_gate = _layer_weights(w_in[l])
        hg_nat, hg_t = _head_gains(moba_q_norm[l], moba_k_norm[l], swa_q_norm[l], swa_k_norm[l],
                                   diff_q_norm[l], diff_k_norm[l])
        nat, q_t, v_t, kmean = _proj(xs, mix_norm[l][None], w_nat, w_t, hg_nat, hg_t, cos, sin, cos_t, sin_t)
        oa = _moba(nat, q_t, v_t, kmean.reshape(seq // MOBA_BLOCK, 2 * LANES))
        ob = _swa(nat, swa_sinks[l])
        oc = _diff(nat, q_t, v_t, diff_lambda_q1[l][None], diff_lambda_k1[l][None], diff_lambda_q2[l][None],
                   diff_lambda_k2[l][None], diff_subln[l][None], lam_init)
        xs = _merge(xs, mix_norm[l][None], w_gate, oa, ob, oc, *out_w, l)
        xs = _ffn(xs, ffn2_norm[l][None], *ffn2, l)
    return xs[None]
```

```python
import functools
import math

import jax
import jax.numpy as jnp
from jax import lax
from jax.experimental import pallas as pl
from jax.experimental.pallas import tpu as pltpu

D_MODEL = 1024
D_FF = 2816
HEAD_DIM = 64
HALF = HEAD_DIM // 2
LANES = 128
MOBA_BLOCK = 256
MOBA_TOPK = 3
SWA_WINDOW = 128
N_BRANCH = 3
ROPE_THETA = 10000.0
EPS = 1e-6
NEG = -1e30
FFN_HALF = 0.5

NAT_QK_COLS = 1280
NAT_COLS = 1536
NAT_KA, NAT_KC, NAT_QB, NAT_KB, NAT_VB = 0, 2, 6, 8, 10
QT_ROWS = 768
VT_ROWS = 768
GATE_COLS = N_BRANCH * D_MODEL

FFN_TILE = 512
ROW_TILE = 512
MERGE_TILE = 1024
MOBA_TILE = 512
DIFF_TILE = 512
GROUP_BLOCKS = 4
KEY_GROUP = GROUP_BLOCKS * MOBA_BLOCK
STAGES_PER_TRIP = 4
PARK_LEAD = 1
SUM_ROWS = 16
SWA_TILE = 4096
VMEM_LIMIT = 48 * 1024 * 1024

f32 = jnp.float32
bf16 = jnp.bfloat16

_NT = (((1,), (1,)), ((), ()))


def _rms(x, gain):
    return x * lax.rsqrt(jnp.mean(x * x, axis=-1, keepdims=True) + EPS) * gain


def _params(n_axes):
    return pltpu.CompilerParams(
        dimension_semantics=("arbitrary",) * n_axes, vmem_limit_bytes=VMEM_LIMIT)


def _const_spec(shape):
    return pl.BlockSpec(shape, lambda *_: (0,) * len(shape), pipeline_mode=pl.Buffered(1))


def _ffn_kernel(x_ref, g_ref, wg_ref, wu_ref, wd_ref, o_ref):
    x = x_ref[...]
    h = _rms(x, g_ref[...]).astype(bf16)
    a = jnp.dot(h, wg_ref[...], preferred_element_type=f32)
    b = jnp.dot(h, wu_ref[...], preferred_element_type=f32)
    act = (a * jax.nn.sigmoid(a) * b).astype(bf16)
    o_ref[...] = x + FFN_HALF * jnp.dot(act, wd_ref[...], preferred_element_type=f32)


def _layer_spec(shape, layer):
    return pl.BlockSpec((None,) + tuple(shape), lambda *_: (layer,) + (0,) * len(shape),
                        pipeline_mode=pl.Buffered(1))


def _ffn(x, gain, wg, wu, wd, layer):
    s = x.shape[0]
    row = pl.BlockSpec((FFN_TILE, D_MODEL), lambda i: (i, 0))
    return pl.pallas_call(
        _ffn_kernel,
        grid=(s // FFN_TILE,),
        in_specs=[row, _const_spec((1, D_MODEL)), _layer_spec((D_MODEL, D_FF), layer),
                  _layer_spec((D_MODEL, D_FF), layer), _layer_spec((D_FF, D_MODEL), layer)],
        out_specs=row,
        out_shape=jax.ShapeDtypeStruct(x.shape, f32),
        compiler_params=_params(1),
        name="ffn",
    )(x, gain, wg, wu, wd)


def _proj_kernel(x_ref, g_ref, wn_ref, wt_ref, hg_ref, hgt_ref, cos_ref, sin_ref, cost_ref, sint_ref,
                 nat_ref, qt_ref, vt_ref, km_ref):
    x = x_ref[...]
    h = _rms(x, g_ref[...]).astype(bf16)

    proj = jnp.dot(h, wn_ref[...], preferred_element_type=f32)
    lane = lax.broadcasted_iota(jnp.int32, (1, LANES), 1)
    first_head = lane < HEAD_DIM
    first_half = (lane % HEAD_DIM) < HALF
    cos = cos_ref[...]
    sin = sin_ref[...]
    for c in range(NAT_QK_COLS // LANES):
        cols = slice(c * LANES, (c + 1) * LANES)
        y = proj[:, cols]
        sq = y * y
        ss0 = jnp.sum(jnp.where(first_head, sq, 0.0), axis=-1, keepdims=True)
        ss1 = jnp.sum(jnp.where(first_head, 0.0, sq), axis=-1, keepdims=True)
        inv = jnp.where(first_head, lax.rsqrt(ss0 / HEAD_DIM + EPS), lax.rsqrt(ss1 / HEAD_DIM + EPS))
        y = y * inv * hg_ref[:, cols]
        partner = jnp.where(first_half, pltpu.roll(y, LANES - HALF, 1), pltpu.roll(y, HALF, 1))
        y = y * cos + partner * sin
        nat_ref[:, cols] = y.astype(bf16)
        if c in (NAT_KA, NAT_KA + 1):
            for b in range(ROW_TILE // MOBA_BLOCK):
                blk = y[b * MOBA_BLOCK:(b + 1) * MOBA_BLOCK]
                km_ref[b, :, (c - NAT_KA) * LANES:(c - NAT_KA + 1) * LANES] = jnp.mean(blk, axis=0, keepdims=True)
    nat_ref[:, NAT_QK_COLS:] = proj[:, NAT_QK_COLS:].astype(bf16)

    proj_t = lax.dot_general(wt_ref[...], h, _NT, preferred_element_type=f32)
    cos_t = cost_ref[...]
    sin_t = sint_ref[...]
    for c in range(QT_ROWS // HEAD_DIM):
        rows = slice(c * HEAD_DIM, (c + 1) * HEAD_DIM)
        y = proj_t[rows, :]
        inv = lax.rsqrt(jnp.mean(y * y, axis=0, keepdims=True) + EPS)
        y = y * inv * hgt_ref[rows, :]
        y1, y2 = y[:HALF], y[HALF:]
        qt_ref[c * HEAD_DIM:c * HEAD_DIM + HALF, :] = (y1 * cos_t - y2 * sin_t).astype(bf16)
        qt_ref[c * HEAD_DIM + HALF:(c + 1) * HEAD_DIM, :] = (y2 * cos_t + y1 * sin_t).astype(bf16)
    vt_ref[...] = proj_t[QT_ROWS:, :].astype(bf16)


def _proj(x, gain, w_nat, w_t, hg_nat, hg_t, cos, sin, cos_t, sin_t):
    s = x.shape[0]
    nb = ROW_TILE // MOBA_BLOCK
    return pl.pallas_call(
        _proj_kernel,
        grid=(s // ROW_TILE,),
        in_specs=[pl.BlockSpec((ROW_TILE, D_MODEL), lambda i: (i, 0)),
                  _const_spec((1, D_MODEL)),
                  _const_spec((D_MODEL, NAT_COLS)),
                  _const_spec((QT_ROWS + VT_ROWS, D_MODEL)),
                  _const_spec((1, NAT_QK_COLS)),
                  _const_spec((QT_ROWS, 1)),
                  pl.BlockSpec((ROW_TILE, LANES), lambda i: (i, 0)),
                  pl.BlockSpec((ROW_TILE, LANES), lambda i: (i, 0)),
                  pl.BlockSpec((HALF, ROW_TILE), lambda i: (0, i)),
                  pl.BlockSpec((HALF, ROW_TILE), lambda i: (0, i))],
        out_specs=[pl.BlockSpec((ROW_TILE, NAT_COLS), lambda i: (i, 0)),
                   pl.BlockSpec((QT_ROWS, ROW_TILE), lambda i: (0, i)),
                   pl.BlockSpec((VT_ROWS, ROW_TILE), lambda i: (0, i)),
                   pl.BlockSpec((nb, 1, 2 * LANES), lambda i: (i, 0, 0))],
        out_shape=[jax.ShapeDtypeStruct((s, NAT_COLS), bf16),
                   jax.ShapeDtypeStruct((QT_ROWS, s), bf16),
                   jax.ShapeDtypeStruct((VT_ROWS, s), bf16),
                   jax.ShapeDtypeStruct((s // MOBA_BLOCK, 1, 2 * LANES), f32)],
        compiler_params=_params(1),
        name="qkv_proj",
    )(x, gain, w_nat, w_t, hg_nat, hg_t, cos, sin, cos_t, sin_t)


def _split_rows(q_t):
    zero = jnp.zeros((HEAD_DIM, q_t.shape[1]), q_t.dtype)
    return (jnp.concatenate([q_t[:HEAD_DIM], zero], axis=0),
            jnp.concatenate([zero, q_t[HEAD_DIM:]], axis=0))


class _Slot:
    def __init__(self, s_ref, mx_ref):
        self.s, self.mx = s_ref, mx_ref


def _with_ones(v_t):
    return jnp.concatenate([v_t, jnp.ones((SUM_ROWS, v_t.shape[1]), v_t.dtype)], axis=0)


def _flash_stage(n_maps, state, score=None, nxt=None, cur=None, value=None):
    m_ref, acc_ref = state
    maps = range(n_maps)
    if cur is not None:
        m_old = [m_ref[h] for h in maps]
        m_new = [jnp.maximum(m_old[h], cur.mx[h]) for h in maps]
        for h in maps:
            acc_ref[h] = jnp.exp2(m_old[h] - m_new[h]) * acc_ref[h]
            m_ref[h] = m_new[h]
    col_max = [None] * n_maps

    def park(b):
        rows = slice(b * MOBA_BLOCK, (b + 1) * MOBA_BLOCK)
        for h in maps:
            s_b = score(b, h)
            nxt.s[h, rows, :] = s_b
            top = jnp.max(s_b, axis=0, keepdims=True)
            col_max[h] = top if b == 0 else jnp.maximum(col_max[h], top)
        if b == GROUP_BLOCKS - 1:
            for h in maps:
                nxt.mx[h] = col_max[h]

    def consume(b):
        rows = slice(b * MOBA_BLOCK, (b + 1) * MOBA_BLOCK)
        for h in maps:
            p = jnp.exp2(cur.s[h, rows, :] - m_new[h]).astype(bf16)
            acc_ref[h] += jnp.dot(_with_ones(value(b, h)), p, preferred_element_type=f32)

    if cur is not None and nxt is cur:
        for b in range(GROUP_BLOCKS):
            consume(b)
            park(b)
        return
    for step in range(GROUP_BLOCKS + PARK_LEAD):
        if score is not None and step < GROUP_BLOCKS:
            park(step)
        if cur is not None and step >= PARK_LEAD:
            consume(step - PARK_LEAD)


def _flash_pipeline(n_maps, first_tile, own_group, own_score, past_score, next_own, value, slots, state):
    m_ref, acc_ref = state
    m_ref[...] = jnp.full(m_ref.shape, NEG, f32)
    acc_ref[...] = jnp.zeros(acc_ref.shape, f32)

    def stage(base, offset, last):
        i = base + offset
        cur = slots[offset % 2]
        nxt = slots[0] if last else slots[1 - offset % 2]
        group = jnp.where(i == 0, own_group, i - 1)
        score = next_own() if last else (lambda b, h: past_score(i, b, h))
        _flash_stage(n_maps, state, score=score, nxt=nxt, cur=cur, value=lambda b, h: value(group, b, h))

    pl.when(first_tile)(functools.partial(_flash_stage, n_maps, state, score=own_score, nxt=slots[0]))

    def trip(k, carry):
        for offset in range(STAGES_PER_TRIP):
            stage(k * STAGES_PER_TRIP, offset, False)
        return carry

    full_trips = own_group // STAGES_PER_TRIP
    lax.fori_loop(0, full_trips, trip, 0)

    base = full_trips * STAGES_PER_TRIP
    left = own_group - base
    assert STAGES_PER_TRIP == 4
    even_base = base + left // 2 * 2

    def ordinary_pair():
        stage(base, 0, False)
        stage(base, 1, False)

    def ordinary_then_last():
        stage(even_base, 0, False)
        stage(even_base, 1, True)

    pl.when(left >= 2)(ordinary_pair)
    pl.when(left % 2 == 1)(ordinary_then_last)
    pl.when(left % 2 == 0)(functools.partial(stage, even_base, 0, True))


def _flash_scratch(n_maps, dv, tq):
    row = pltpu.VMEM((n_maps, 1, tq), f32)
    scores = pltpu.VMEM((n_maps, KEY_GROUP, tq), f32)
    return [row, pltpu.VMEM((n_maps, dv + SUM_ROWS, tq), f32), scores, row, scores, row]


def _positions(first_key, n_keys, t, tq):
    key = first_key + lax.broadcasted_iota(jnp.int32, (n_keys, tq), 0)
    query = t * tq + lax.broadcasted_iota(jnp.int32, (n_keys, tq), 1)
    return key, query


def _moba_kernel(qt_ref, qn_ref, k_ref, vt_ref, km_ref, o_ref, m_ref, acc_ref, sa_ref, mxa_ref, sb_ref, mxb_ref,
                 picks_ref):
    t = pl.program_id(1)
    t_next = jnp.minimum(t + 1, pl.num_programs(1) - 1)
    n_blocks = km_ref.shape[0]

    km = km_ref[...]
    km_hi = km.astype(bf16)
    rem = km - km_hi.astype(f32)
    km_mid = rem.astype(bf16)
    km_lo = (rem - km_mid.astype(f32)).astype(bf16)
    blk = lax.broadcasted_iota(jnp.int32, (n_blocks, 1), 0).astype(f32)

    def block_rows(g, b):
        return pl.ds(pl.multiple_of((g * GROUP_BLOCKS + b) * MOBA_BLOCK, MOBA_BLOCK), MOBA_BLOCK)

    def query_row(tile):
        return tile * MOBA_TILE + lax.broadcasted_iota(jnp.int32, (1, MOBA_TILE), 1)

    def pick_blocks(tile, q_heads):
        own_blk = (query_row(tile) // MOBA_BLOCK).astype(f32)
        picks = []
        for qh in q_heads:
            gate = (jnp.dot(km_hi, qh, preferred_element_type=f32)
                    + jnp.dot(km_mid, qh, preferred_element_type=f32)
                    + jnp.dot(km_lo, qh, preferred_element_type=f32))
            gate = jnp.where(blk < own_blk, gate, NEG)
            chosen = []
            for _ in range(MOBA_TOPK):
                best = jnp.max(gate, axis=0, keepdims=True)
                idx = jnp.min(jnp.where(gate == best, blk, 1e9), axis=0, keepdims=True)
                chosen.append(jnp.where(best > 0.5 * NEG, idx, -1.0))
                gate = jnp.where(blk == idx, 2.0 * NEG, gate)
            picks.append(chosen)
        return picks

    def tile_scores(tile, q_heads, picks):
        def picked(g, b, h):
            blk_f = (g * GROUP_BLOCKS + b).astype(f32)
            return (picks[h][0] == blk_f) | (picks[h][1] == blk_f) | (picks[h][2] == blk_f)

        def past_score(g, b, h):
            bias = jnp.where(picked(g, b, h), 0.0, NEG)
            first_row = lax.broadcasted_iota(jnp.int32, (SUM_ROWS, MOBA_TILE), 0) == 0
            bias_rows = jnp.where(first_row, bias, 0.0).astype(bf16)
            rest = jnp.zeros((LANES - SUM_ROWS, MOBA_TILE), bf16)
            rhs = jnp.concatenate([q_heads[h], bias_rows, rest], axis=0)
            lhs = jnp.concatenate([k_ref[block_rows(g, b), :], jnp.ones((MOBA_BLOCK, LANES), bf16)], axis=1)
            return jnp.dot(lhs, rhs, preferred_element_type=f32)

        def own_score(b, h):
            group = tile * MOBA_TILE // KEY_GROUP
            s_b = jnp.dot(k_ref[block_rows(group, b), :], q_heads[h], preferred_element_type=f32)
            key, query = _positions((group * GROUP_BLOCKS + b) * MOBA_BLOCK, MOBA_BLOCK, tile, MOBA_TILE)
            own_start = query_row(tile) // MOBA_BLOCK * MOBA_BLOCK
            masked = jnp.where(picked(group, b, h), s_b, NEG)
            return jnp.where(key >= own_start, jnp.where(key <= query, s_b, NEG), masked)

        return own_score, past_score

    def value(g, b, h):
        return vt_ref[h * HEAD_DIM:(h + 1) * HEAD_DIM, block_rows(g, b)]

    def store_picks(picks):
        for h in range(2):
            for i in range(MOBA_TOPK):
                picks_ref[h * MOBA_TOPK + i:h * MOBA_TOPK + i + 1, :] = picks[h][i]

    q_heads = _split_rows(qt_ref[...])
    pl.when(t == 0)(lambda: store_picks(pick_blocks(t, q_heads)))
    picks = [[picks_ref[h * MOBA_TOPK + i:h * MOBA_TOPK + i + 1, :] for i in range(MOBA_TOPK)] for h in range(2)]
    own_score, past_score = tile_scores(t, q_heads, picks)

    def next_own():
        q_next = _split_rows(qn_ref[...])
        picks_next = pick_blocks(t_next, q_next)
        store_picks(picks_next)
        return tile_scores(t_next, q_next, picks_next)[0]

    _flash_pipeline(2, t == 0, t * MOBA_TILE // KEY_GROUP, own_score, past_score, next_own, value,
                    (_Slot(sa_ref, mxa_ref), _Slot(sb_ref, mxb_ref)), (m_ref, acc_ref))

    heads = [acc_ref[h, :HEAD_DIM, :] * (1.0 / acc_ref[h, HEAD_DIM:HEAD_DIM + 1, :]) for h in range(2)]
    o_ref[...] = jnp.concatenate(heads, axis=0).T.astype(bf16)


def _moba(nat, q_t, v_t, kmean):
    s = nat.shape[0]
    n_blocks = s // MOBA_BLOCK
    last = s // MOBA_TILE - 1
    return pl.pallas_call(
        _moba_kernel,
        grid=(2, s // MOBA_TILE),
        in_specs=[pl.BlockSpec((LANES, MOBA_TILE), lambda p, t: (p, t)),
                  pl.BlockSpec((LANES, MOBA_TILE), lambda p, t: (p, jnp.minimum(t + 1, last))),
                  pl.BlockSpec((s, LANES), lambda p, t: (0, NAT_KA + p)),
                  pl.BlockSpec((LANES, s), lambda p, t: (p, 0)),
                  pl.BlockSpec((n_blocks, LANES), lambda p, t: (0, p))],
        out_specs=pl.BlockSpec((MOBA_TILE, LANES), lambda p, t: (t, p)),
        out_shape=jax.ShapeDtypeStruct((s, 2 * LANES), bf16),
        scratch_shapes=_flash_scratch(2, HEAD_DIM, MOBA_TILE) + [pltpu.VMEM((2 * MOBA_TOPK, MOBA_TILE), f32)],
        compiler_params=_params(2),
        name="moba_attention",
    )(q_t, q_t, nat, v_t, kmean)


def _swa_kernel(sink_ref, q_ref, k_ref, kp_ref, v_ref, vp_ref, o_ref):
    p_id = pl.program_id(0)
    t = pl.program_id(1)
    sub_tiles = SWA_TILE // SWA_WINDOW
    lane = lax.broadcasted_iota(jnp.int32, (1, LANES), 1)
    zero = jnp.zeros((SWA_WINDOW, LANES), bf16)
    qi = lax.broadcasted_iota(jnp.int32, (SWA_WINDOW, 2 * SWA_WINDOW), 0)
    kj = lax.broadcasted_iota(jnp.int32, (SWA_WINDOW, 2 * SWA_WINDOW), 1)
    back = qi + SWA_WINDOW - kj
    in_window = (back >= 0) & (back < SWA_WINDOW)
    for sub in range(sub_tiles):
        rows = slice(sub * SWA_WINDOW, (sub + 1) * SWA_WINDOW)
        if sub == 0:
            k_prev, v_prev = kp_ref[...], vp_ref[...]
            mask = in_window & (kj >= SWA_WINDOW - t * SWA_TILE)
        else:
            prev = slice((sub - 1) * SWA_WINDOW, sub * SWA_WINDOW)
            k_prev, v_prev = k_ref[prev, :], v_ref[prev, :]
            mask = in_window
        k_band = jnp.concatenate([k_prev, k_ref[rows, :]], axis=0)
        v_band = jnp.concatenate([v_prev, v_ref[rows, :]], axis=0)
        q = q_ref[rows, :]
        outs = []
        for h, qh in enumerate((jnp.where(lane < HEAD_DIM, q, zero), jnp.where(lane < HEAD_DIM, zero, q))):
            sink = sink_ref[2 * p_id + h]
            s = lax.dot_general(qh, k_band, _NT, preferred_element_type=f32)
            s = jnp.where(mask, s, NEG)
            m = jnp.maximum(jnp.max(s, axis=-1, keepdims=True), sink)
            p = jnp.exp(s - m)
            denom = jnp.sum(p, axis=-1, keepdims=True) + jnp.exp(sink - m)
            outs.append(jnp.dot(p.astype(bf16), v_band, preferred_element_type=f32) / denom)
        o_ref[rows, :] = jnp.where(lane < HEAD_DIM, outs[0], outs[1]).astype(bf16)


def _swa(nat, sinks):
    s = nat.shape[0]
    ratio = SWA_TILE // SWA_WINDOW

    def prev_rows(t):
        return jnp.maximum(t * ratio - 1, 0)

    return pl.pallas_call(
        _swa_kernel,
        grid=(2, s // SWA_TILE),
        in_specs=[pl.BlockSpec(memory_space=pltpu.SMEM),
                  pl.BlockSpec((SWA_TILE, LANES), lambda p, t: (t, NAT_QB + p)),
                  pl.BlockSpec((SWA_TILE, LANES), lambda p, t: (t, NAT_KB + p)),
                  pl.BlockSpec((SWA_WINDOW, LANES), lambda p, t: (prev_rows(t), NAT_KB + p)),
                  pl.BlockSpec((SWA_TILE, LANES), lambda p, t: (t, NAT_VB + p)),
                  pl.BlockSpec((SWA_WINDOW, LANES), lambda p, t: (prev_rows(t), NAT_VB + p))],
        out_specs=pl.BlockSpec((SWA_TILE, LANES), lambda p, t: (t, p)),
        out_shape=jax.ShapeDtypeStruct((s, 2 * LANES), bf16),
        compiler_params=_params(2),
        name="swa_attention",
    )(sinks, nat, nat, nat, nat, nat)


def _diff_kernel(lam_init, qt_ref, qn_ref, k_ref, vt_ref, lq1_ref, lk1_ref, lq2_ref, lk2_ref, sub_ref,
                 o_ref, m_ref, acc_ref, sa_ref, mxa_ref, sb_ref, mxb_ref):
    t = pl.program_id(1)
    t_next = jnp.minimum(t + 1, pl.num_programs(1) - 1)

    def block_rows(g, b):
        return pl.ds(pl.multiple_of((g * GROUP_BLOCKS + b) * MOBA_BLOCK, MOBA_BLOCK), MOBA_BLOCK)

    def tile_scores(tile, q_ref):
        q_maps = _split_rows(q_ref[...])

        def past_score(g, b, h):
            return jnp.dot(k_ref[block_rows(g, b), :], q_maps[h], preferred_element_type=f32)

        def own_score(b, h):
            group = tile * DIFF_TILE // KEY_GROUP
            key, query = _positions((group * GROUP_BLOCKS + b) * MOBA_BLOCK, MOBA_BLOCK, tile, DIFF_TILE)
            return jnp.where(key <= query, past_score(group, b, h), NEG)

        return own_score, past_score

    def value(g, b, h):
        return vt_ref[:, block_rows(g, b)]

    own_score, past_score = tile_scores(t, qt_ref)
    _flash_pipeline(2, t == 0, t * DIFF_TILE // KEY_GROUP, own_score, past_score,
                    lambda: tile_scores(t_next, qn_ref)[0], value,
                    (_Slot(sa_ref, mxa_ref), _Slot(sb_ref, mxb_ref)), (m_ref, acc_ref))

    lam = (jnp.exp(jnp.sum(lq1_ref[...] * lk1_ref[...], axis=-1, keepdims=True))
           - jnp.exp(jnp.sum(lq2_ref[...] * lk2_ref[...], axis=-1, keepdims=True)) + lam_init)
    maps = [acc_ref[h, :LANES, :] * (1.0 / acc_ref[h, LANES:LANES + 1, :]) for h in range(2)]
    out_t = maps[0] - lam * maps[1]
    inv = lax.rsqrt(jnp.mean(out_t * out_t, axis=0, keepdims=True) + EPS) * (1.0 - lam_init)
    o_ref[...] = ((out_t * inv).T * sub_ref[...]).astype(bf16)


def _diff(nat, q_t, v_t, lq1, lk1, lq2, lk2, subln, lam_init):
    s = nat.shape[0]
    vec = _const_spec((1, HEAD_DIM))
    last = s // DIFF_TILE - 1
    return pl.pallas_call(
        functools.partial(_diff_kernel, lam_init),
        grid=(4, s // DIFF_TILE),
        in_specs=[pl.BlockSpec((LANES, DIFF_TILE), lambda h, t: (2 + h, t)),
                  pl.BlockSpec((LANES, DIFF_TILE), lambda h, t: (2 + h, jnp.minimum(t + 1, last))),
                  pl.BlockSpec((s, LANES), lambda h, t: (0, NAT_KC + h)),
                  pl.BlockSpec((LANES, s), lambda h, t: (2 + h, 0)),
                  vec, vec, vec, vec, _const_spec((1, LANES))],
        out_specs=pl.BlockSpec((DIFF_TILE, LANES), lambda h, t: (t, h)),
        out_shape=jax.ShapeDtypeStruct((s, 4 * LANES), bf16),
        scratch_shapes=_flash_scratch(2, LANES, DIFF_TILE),
        compiler_params=_params(2),
        name="diff_attention",
    )(q_t, q_t, nat, v_t, lq1, lk1, lq2, lk2, subln)


def _merge_kernel(x_ref, g_ref, wgate_ref, oa_ref, ob_ref, oc_ref, pa_ref, pb_ref, pc_ref, wo_ref, o_ref):
    x = x_ref[...]
    h = _rms(x, g_ref[...]).astype(bf16)
    branches = ((oa_ref, pa_ref), (ob_ref, pb_ref), (oc_ref, pc_ref))
    merged = jnp.zeros_like(x)
    for i, (b_ref, p_ref) in enumerate(branches):
        logits = jnp.dot(h, wgate_ref[:, i * D_MODEL:(i + 1) * D_MODEL], preferred_element_type=f32)
        y = jnp.dot(b_ref[...], p_ref[...], preferred_element_type=f32)
        merged = merged + jax.nn.sigmoid(logits) * y
    o_ref[...] = x + jnp.dot(merged.astype(bf16), wo_ref[...], preferred_element_type=f32)


def _merge(x, gain, w_gate, oa, ob, oc, pa, pb, pc, w_out, layer):
    s = x.shape[0]

    def rows(width):
        return pl.BlockSpec((MERGE_TILE, width), lambda i: (i, 0))

    return pl.pallas_call(
        _merge_kernel,
        grid=(s // MERGE_TILE,),
        in_specs=[rows(D_MODEL), _const_spec((1, D_MODEL)), _const_spec((D_MODEL, GATE_COLS)),
                  rows(oa.shape[1]), rows(ob.shape[1]), rows(oc.shape[1]),
                  _layer_spec(pa.shape[1:], layer), _layer_spec(pb.shape[1:], layer),
                  _layer_spec(pc.shape[1:], layer), _layer_spec((D_MODEL, D_MODEL), layer)],
        out_specs=rows(D_MODEL),
        out_shape=jax.ShapeDtypeStruct(x.shape, f32),
        compiler_params=_params(1),
        name="gated_merge",
    )(x, gain, w_gate, oa, ob, oc, pa, pb, pc, w_out)


def _rope_tables(seq):
    pos = jnp.arange(seq, dtype=f32)
    inv_freq = ROPE_THETA ** (-jnp.arange(0, HEAD_DIM, 2, dtype=f32) / HEAD_DIM)
    ang = pos[:, None] * inv_freq[None, :]
    cos, sin = jnp.cos(ang), jnp.sin(ang)
    return (jnp.tile(cos, (1, 4)), jnp.concatenate([-sin, sin, -sin, sin], axis=-1), cos.T, sin.T)


def _layer_weights(w_in):
    qa, ka, va = w_in[:, 0:256], w_in[:, 256:512], w_in[:, 512:768]
    qb, kb, vb = w_in[:, 768:1024], w_in[:, 1024:1152], w_in[:, 1152:1280]
    qc, kc, vc = w_in[:, 1280:1792], w_in[:, 1792:2304], w_in[:, 2304:2816]
    gates = w_in[:, 2816:]

    def dup(w):
        h0, h1 = w[:, :HEAD_DIM], w[:, HEAD_DIM:]
        return jnp.concatenate([h0, h0, h1, h1], axis=1)

    w_nat = jnp.concatenate([ka, kc, qb, dup(kb), dup(vb)], axis=1)
    w_t = jnp.concatenate([qa, qc, va, vc], axis=1).T
    return w_nat, w_t, gates


def _head_gains(qa, ka, qb, kb, qc, kc):
    scale = HEAD_DIM ** -0.5
    scale2 = scale * math.log2(math.e)
    nat = jnp.concatenate([jnp.tile(ka, 4), jnp.tile(kc, 8), jnp.tile(qb * scale, 4), jnp.tile(kb, 4)])
    feat = jnp.concatenate([jnp.tile(qa * scale2, 4), jnp.tile(qc * scale2, 8)])
    return nat[None, :], feat[:, None]


def kernel(x, ffn1_norm, ffn1_w_gate, ffn1_w_up, ffn1_w_down, mix_norm, w_in, moba_q_norm, moba_k_norm, swa_q_norm, swa_k_norm, swa_sinks, diff_q_norm, diff_k_norm, diff_lambda_q1, diff_lambda_k1, diff_lambda_q2, diff_lambda_k2, diff_subln, w_branch_a, w_branch_b, w_branch_c, w_out, ffn2_norm, ffn2_w_gate, ffn2_w_up, ffn2_w_down):
    batch, seq, _ = x.shape
    assert batch == 1 and seq % SWA_TILE == 0 and seq % ROW_TILE == 0 and seq % KEY_GROUP == 0
    depth = w_in.shape[0]
    cos, sin, cos_t, sin_t = _rope_tables(seq)
    ffn1 = [w.astype(bf16) for w in (ffn1_w_gate, ffn1_w_up, ffn1_w_down)]
    ffn2 = [w.astype(bf16) for w in (ffn2_w_gate, ffn2_w_up, ffn2_w_down)]
    out_w = [w.astype(bf16) for w in (w_branch_a, w_branch_b, w_branch_c, w_out)]
    w_in = w_in.astype(bf16)
    xs = x[0]
    for l in range(depth):
        lam_init = 0.8 - 0.6 * math.exp(-0.3 * l)
        xs = _ffn(xs, ffn1_norm[l][None], *ffn1, l)
        w_nat, w_t, w_gate = _layer_weights(w_in[l])
        hg_nat, hg_t = _head_gains(moba_q_norm[l], moba_k_norm[l], swa_q_norm[l], swa_k_norm[l],
                                   diff_q_norm[l], diff_k_norm[l])
        nat, q_t, v_t, kmean = _proj(xs, mix_norm[l][None], w_nat, w_t, hg_nat, hg_t, cos, sin, cos_t, sin_t)
        oa = _moba(nat, q_t, v_t, kmean.reshape(seq // MOBA_BLOCK, 2 * LANES))
        ob = _swa(nat, swa_sinks[l])
        oc = _diff(nat, q_t, v_t, diff_lambda_q1[l][None], diff_lambda_k1[l][None], diff_lambda_q2[l][None],
                   diff_lambda_k2[l][None], diff_subln[l][None], lam_init)
        xs = _merge(xs, mix_norm[l][None], w_gate, oa, ob, oc, *out_w, l)
        xs = _ffn(xs, ffn2_norm[l][None], *ffn2, l)
    return xs[None]
```

```python
import functools
import math

import jax
import jax.numpy as jnp
from jax import lax
from jax.experimental import pallas as pl
from jax.experimental.pallas import tpu as pltpu

D_MODEL = 1024
D_FF = 2816
HEAD_DIM = 64
HALF = HEAD_DIM // 2
LANES = 128
MOBA_BLOCK = 256
MOBA_TOPK = 3
SWA_WINDOW = 128
N_BRANCH = 3
ROPE_THETA = 10000.0
EPS = 1e-6
NEG = -1e30
FFN_HALF = 0.5

NAT_QK_COLS = 1280
NAT_COLS = 1536
NAT_KA, NAT_KC, NAT_QB, NAT_KB, NAT_VB = 0, 2, 6, 8, 10
QT_ROWS = 768
VT_ROWS = 768
GATE_COLS = N_BRANCH * D_MODEL

FFN_TILE = 512
ROW_TILE = 512
MERGE_TILE = 1024
MOBA_TILE = 512
DIFF_TILE = 512
GROUP_BLOCKS = 4
KEY_GROUP = GROUP_BLOCKS * MOBA_BLOCK
STAGES_PER_TRIP = 4
PARK_LEAD = 1
SUM_ROWS = 16
SWA_TILE = 4096
VMEM_LIMIT = 48 * 1024 * 1024

f32 = jnp.float32
bf16 = jnp.bfloat16

_NT = (((1,), (1,)), ((), ()))


def _rms(x, gain):
    return x * lax.rsqrt(jnp.mean(x * x, axis=-1, keepdims=True) + EPS) * gain


def _params(n_axes):
    return pltpu.CompilerParams(
        dimension_semantics=("arbitrary",) * n_axes, vmem_limit_bytes=VMEM_LIMIT)


def _const_spec(shape):
    return pl.BlockSpec(shape, lambda *_: (0,) * len(shape), pipeline_mode=pl.Buffered(1))


def _ffn_kernel(x_ref, g_ref, wg_ref, wu_ref, wd_ref, o_ref):
    x = x_ref[...]
    h = _rms(x, g_ref[...]).astype(bf16)
    a = jnp.dot(h, wg_ref[...], preferred_element_type=f32)
    b = jnp.dot(h, wu_ref[...], preferred_element_type=f32)
    act = (a * jax.nn.sigmoid(a) * b).astype(bf16)
    o_ref[...] = x + FFN_HALF * jnp.dot(act, wd_ref[...], preferred_element_type=f32)


def _layer_spec(shape, layer):
    return pl.BlockSpec((None,) + tuple(shape), lambda *_: (layer,) + (0,) * len(shape),
                        pipeline_mode=pl.Buffered(1))


def _ffn(x, gain, wg, wu, wd, layer):
    s = x.shape[0]
    row = pl.BlockSpec((FFN_TILE, D_MODEL), lambda i: (i, 0))
    return pl.pallas_call(
        _ffn_kernel,
        grid=(s // FFN_TILE,),
        in_specs=[row, _const_spec((1, D_MODEL)), _layer_spec((D_MODEL, D_FF), layer),
                  _layer_spec((D_MODEL, D_FF), layer), _layer_spec((D_FF, D_MODEL), layer)],
        out_specs=row,
        out_shape=jax.ShapeDtypeStruct(x.shape, f32),
        compiler_params=_params(1),
        name="ffn",
    )(x, gain, wg, wu, wd)


def _proj_kernel(x_ref, g_ref, wn_ref, wt_ref, hg_ref, hgt_ref, cos_ref, sin_ref, cost_ref, sint_ref,
                 nat_ref, qt_ref, vt_ref, km_ref):
    x = x_ref[...]
    h = _rms(x, g_ref[...]).astype(bf16)

    proj = jnp.dot(h, wn_ref[...], preferred_element_type=f32)
    lane = lax.broadcasted_iota(jnp.int32, (1, LANES), 1)
    first_head = lane < HEAD_DIM
    first_half = (lane % HEAD_DIM) < HALF
    cos = cos_ref[...]
    sin = sin_ref[...]
    for c in range(NAT_QK_COLS // LANES):
        cols = slice(c * LANES, (c + 1) * LANES)
        y = proj[:, cols]
        sq = y * y
        ss0 = jnp.sum(jnp.where(first_head, sq, 0.0), axis=-1, keepdims=True)
        ss1 = jnp.sum(jnp.where(first_head, 0.0, sq), axis=-1, keepdims=True)
        inv = jnp.where(first_head, lax.rsqrt(ss0 / HEAD_DIM + EPS), lax.rsqrt(ss1 / HEAD_DIM + EPS))
        y = y * inv * hg_ref[:, cols]
        partner = jnp.where(first_half, pltpu.roll(y, LANES - HALF, 1), pltpu.roll(y, HALF, 1))
        y = y * cos + partner * sin
        nat_ref[:, cols] = y.astype(bf16)
        if c in (NAT_KA, NAT_KA + 1):
            for b in range(ROW_TILE // MOBA_BLOCK):
                blk = y[b * MOBA_BLOCK:(b + 1) * MOBA_BLOCK]
                km_ref[b, :, (c - NAT_KA) * LANES:(c - NAT_KA + 1) * LANES] = jnp.mean(blk, axis=0, keepdims=True)
    nat_ref[:, NAT_QK_COLS:] = proj[:, NAT_QK_COLS:].astype(bf16)

    proj_t = lax.dot_general(wt_ref[...], h, _NT, preferred_element_type=f32)
    cos_t = cost_ref[...]
    sin_t = sint_ref[...]
    for c in range(QT_ROWS // HEAD_DIM):
        rows = slice(c * HEAD_DIM, (c + 1) * HEAD_DIM)
        y = proj_t[rows, :]
        inv = lax.rsqrt(jnp.mean(y * y, axis=0, keepdims=True) + EPS)
        y = y * inv * hgt_ref[rows, :]
        y1, y2 = y[:HALF], y[HALF:]
        qt_ref[c * HEAD_DIM:c * HEAD_DIM + HALF, :] = (y1 * cos_t - y2 * sin_t).astype(bf16)
        qt_ref[c * HEAD_DIM + HALF:(c + 1) * HEAD_DIM, :] = (y2 * cos_t + y1 * sin_t).astype(bf16)
    vt_ref[...] = proj_t[QT_ROWS:, :].astype(bf16)


def _proj(x, gain, w_nat, w_t, hg_nat, hg_t, cos, sin, cos_t, sin_t):
    s = x.shape[0]
    nb = ROW_TILE // MOBA_BLOCK
    return pl.pallas_call(
        _proj_kernel,
        grid=(s // ROW_TILE,),
        in_specs=[pl.BlockSpec((ROW_TILE, D_MODEL), lambda i: (i, 0)),
                  _const_spec((1, D_MODEL)),
                  _const_spec((D_MODEL, NAT_COLS)),
                  _const_spec((QT_ROWS + VT_ROWS, D_MODEL)),
                  _const_spec((1, NAT_QK_COLS)),
                  _const_spec((QT_ROWS, 1)),
                  pl.BlockSpec((ROW_TILE, LANES), lambda i: (i, 0)),
                  pl.BlockSpec((ROW_TILE, LANES), lambda i: (i, 0)),
                  pl.BlockSpec((HALF, ROW_TILE), lambda i: (0, i)),
                  pl.BlockSpec((HALF, ROW_TILE), lambda i: (0, i))],
        out_specs=[pl.BlockSpec((ROW_TILE, NAT_COLS), lambda i: (i, 0)),
                   pl.BlockSpec((QT_ROWS, ROW_TILE), lambda i: (0, i)),
                   pl.BlockSpec((VT_ROWS, ROW_TILE), lambda i: (0, i)),
                   pl.BlockSpec((nb, 1, 2 * LANES), lambda i: (i, 0, 0))],
        out_shape=[jax.ShapeDtypeStruct((s, NAT_COLS), bf16),
                   jax.ShapeDtypeStruct((QT_ROWS, s), bf16),
                   jax.ShapeDtypeStruct((VT_ROWS, s), bf16),
                   jax.ShapeDtypeStruct((s // MOBA_BLOCK, 1, 2 * LANES), f32)],
        compiler_params=_params(1),
        name="qkv_proj",
    )(x, gain, w_nat, w_t, hg_nat, hg_t, cos, sin, cos_t, sin_t)


def _split_rows(q_t):
    zero = jnp.zeros((HEAD_DIM, q_t.shape[1]), q_t.dtype)
    return (jnp.concatenate([q_t[:HEAD_DIM], zero], axis=0),
            jnp.concatenate([zero, q_t[HEAD_DIM:]], axis=0))


class _Slot:
    def __init__(self, s_ref, mx_ref):
        self.s, self.mx = s_ref, mx_ref


def _with_ones(v_t):
    return jnp.concatenate([v_t, jnp.ones((SUM_ROWS, v_t.shape[1]), v_t.dtype)], axis=0)


def _flash_stage(n_maps, state, score=None, nxt=None, cur=None, value=None):
    m_ref, acc_ref = state
    maps = range(n_maps)
    if cur is not None:
        m_old = [m_ref[h] for h in maps]
        m_new = [jnp.maximum(m_old[h], cur.mx[h]) for h in maps]
        for h in maps:
            acc_ref[h] = jnp.exp2(m_old[h] - m_new[h]) * acc_ref[h]
            m_ref[h] = m_new[h]
    col_max = [None] * n_maps

    def park(b):
        rows = slice(b * MOBA_BLOCK, (b + 1) * MOBA_BLOCK)
        for h in maps:
            s_b = score(b, h)
            nxt.s[h, rows, :] = s_b
            top = jnp.max(s_b, axis=0, keepdims=True)
            col_max[h] = top if b == 0 else jnp.maximum(col_max[h], top)
        if b == GROUP_BLOCKS - 1:
            for h in maps:
                nxt.mx[h] = col_max[h]

    def consume(b):
        rows = slice(b * MOBA_BLOCK, (b + 1) * MOBA_BLOCK)
        for h in maps:
            p = jnp.exp2(cur.s[h, rows, :] - m_new[h]).astype(bf16)
            acc_ref[h] += jnp.dot(_with_ones(value(b, h)), p, preferred_element_type=f32)

    if cur is not None and nxt is cur:
        for b in range(GROUP_BLOCKS):
            consume(b)
            park(b)
        return
    for step in range(GROUP_BLOCKS + PARK_LEAD):
        if score is not None and step < GROUP_BLOCKS:
            park(step)
        if cur is not None and step >= PARK_LEAD:
            consume(step - PARK_LEAD)


def _flash_pipeline(n_maps, first_tile, own_group, own_score, past_score, next_own, value, slots, state):
    m_ref, acc_ref = state
    m_ref[...] = jnp.full(m_ref.shape, NEG, f32)
    acc_ref[...] = jnp.zeros(acc_ref.shape, f32)

    def stage(base, offset, last):
        i = base + offset
        cur = slots[offset % 2]
        nxt = slots[0] if last else slots[1 - offset % 2]
        group = jnp.where(i == 0, own_group, i - 1)
        score = next_own() if last else (lambda b, h: past_score(i, b, h))
        _flash_stage(n_maps, state, score=score, nxt=nxt, cur=cur, value=lambda b, h: value(group, b, h))

    pl.when(first_tile)(functools.partial(_flash_stage, n_maps, state, score=own_score, nxt=slots[0]))

    def trip(k, carry):
        for offset in range(STAGES_PER_TRIP):
            stage(k * STAGES_PER_TRIP, offset, False)
        return carry

    full_trips = own_group // STAGES_PER_TRIP
    lax.fori_loop(0, full_trips, trip, 0)

    base = full_trips * STAGES_PER_TRIP
    left = own_group - base

    def tail(n_ordinary):
        for offset in range(n_ordinary):
            stage(base, offset, False)
        stage(base, n_ordinary, True)

    for n_ordinary in range(STAGES_PER_TRIP):
        pl.when(left == n_ordinary)(functools.partial(tail, n_ordinary))


def _flash_scratch(n_maps, dv, tq):
    row = pltpu.VMEM((n_maps, 1, tq), f32)
    scores = pltpu.VMEM((n_maps, KEY_GROUP, tq), f32)
    return [row, pltpu.VMEM((n_maps, dv + SUM_ROWS, tq), f32), scores, row, scores, row]


def _positions(first_key, n_keys, t, tq):
    key = first_key + lax.broadcasted_iota(jnp.int32, (n_keys, tq), 0)
    query = t * tq + lax.broadcasted_iota(jnp.int32, (n_keys, tq), 1)
    return key, query


def _moba_kernel(qt_ref, qn_ref, k_ref, vt_ref, km_ref, o_ref, m_ref, acc_ref, sa_ref, mxa_ref, sb_ref, mxb_ref,
                 picks_ref):
    t = pl.program_id(1)
    t_next = jnp.minimum(t + 1, pl.num_programs(1) - 1)
    n_blocks = km_ref.shape[0]

    km = km_ref[...]
    km_hi = km.astype(bf16)
    rem = km - km_hi.astype(f32)
    km_mid = rem.astype(bf16)
    km_lo = (rem - km_mid.astype(f32)).astype(bf16)
    blk = lax.broadcasted_iota(jnp.int32, (n_blocks, 1), 0).astype(f32)

    def block_rows(g, b):
        return pl.ds(pl.multiple_of((g * GROUP_BLOCKS + b) * MOBA_BLOCK, MOBA_BLOCK), MOBA_BLOCK)

    def query_row(tile):
        return tile * MOBA_TILE + lax.broadcasted_iota(jnp.int32, (1, MOBA_TILE), 1)

    def pick_blocks(tile, q_heads):
        own_blk = (query_row(tile) // MOBA_BLOCK).astype(f32)
        picks = []
        for qh in q_heads:
            gate = (jnp.dot(km_hi, qh, preferred_element_type=f32)
                    + jnp.dot(km_mid, qh, preferred_element_type=f32)
                    + jnp.dot(km_lo, qh, preferred_element_type=f32))
            gate = jnp.where(blk < own_blk, gate, NEG)
            chosen = []
            for _ in range(MOBA_TOPK):
                best = jnp.max(gate, axis=0, keepdims=True)
                idx = jnp.min(jnp.where(gate == best, blk, 1e9), axis=0, keepdims=True)
                chosen.append(jnp.where(best > 0.5 * NEG, idx, -1.0))
                gate = jnp.where(blk == idx, 2.0 * NEG, gate)
            picks.append(chosen)
        return picks

    def tile_scores(tile, q_heads, picks):
        def picked(g, b, h):
            blk_f = (g * GROUP_BLOCKS + b).astype(f32)
            return (picks[h][0] == blk_f) | (picks[h][1] == blk_f) | (picks[h][2] == blk_f)

        def past_score(g, b, h):
            bias = jnp.where(picked(g, b, h), 0.0, NEG)
            first_row = lax.broadcasted_iota(jnp.int32, (SUM_ROWS, MOBA_TILE), 0) == 0
            bias_rows = jnp.where(first_row, bias, 0.0).astype(bf16)
            rest = jnp.zeros((LANES - SUM_ROWS, MOBA_TILE), bf16)
            rhs = jnp.concatenate([q_heads[h], bias_rows, rest], axis=0)
            lhs = jnp.concatenate([k_ref[block_rows(g, b), :], jnp.ones((MOBA_BLOCK, LANES), bf16)], axis=1)
            return jnp.dot(lhs, rhs, preferred_element_type=f32)

        def own_score(b, h):
            group = tile * MOBA_TILE // KEY_GROUP
            s_b = jnp.dot(k_ref[block_rows(group, b), :], q_heads[h], preferred_element_type=f32)
            key, query = _positions((group * GROUP_BLOCKS + b) * MOBA_BLOCK, MOBA_BLOCK, tile, MOBA_TILE)
            own_start = query_row(tile) // MOBA_BLOCK * MOBA_BLOCK
            masked = jnp.where(picked(group, b, h), s_b, NEG)
            return jnp.where(key >= own_start, jnp.where(key <= query, s_b, NEG), masked)

        return own_score, past_score

    def value(g, b, h):
        return vt_ref[h * HEAD_DIM:(h + 1) * HEAD_DIM, block_rows(g, b)]

    def store_picks(picks):
        for h in range(2):
            for i in range(MOBA_TOPK):
                picks_ref[h * MOBA_TOPK + i:h * MOBA_TOPK + i + 1, :] = picks[h][i]

    q_heads = _split_rows(qt_ref[...])
    pl.when(t == 0)(lambda: store_picks(pick_blocks(t, q_heads)))
    picks = [[picks_ref[h * MOBA_TOPK + i:h * MOBA_TOPK + i + 1, :] for i in range(MOBA_TOPK)] for h in range(2)]
    own_score, past_score = tile_scores(t, q_heads, picks)

    def next_own():
        q_next = _split_rows(qn_ref[...])
        picks_next = pick_blocks(t_next, q_next)
        store_picks(picks_next)
        return tile_scores(t_next, q_next, picks_next)[0]

    _flash_pipeline(2, t == 0, t * MOBA_TILE // KEY_GROUP, own_score, past_score, next_own, value,
                    (_Slot(sa_ref, mxa_ref), _Slot(sb_ref, mxb_ref)), (m_ref, acc_ref))

    heads = [acc_ref[h, :HEAD_DIM, :] * (1.0 / acc_ref[h, HEAD_DIM:HEAD_DIM + 1, :]) for h in range(2)]
    o_ref[...] = jnp.concatenate(heads, axis=0).T.astype(bf16)


def _moba(nat, q_t, v_t, kmean):
    s = nat.shape[0]
    n_blocks = s // MOBA_BLOCK
    last = s // MOBA_TILE - 1
    return pl.pallas_call(
        _moba_kernel,
        grid=(2, s // MOBA_TILE),
        in_specs=[pl.BlockSpec((LANES, MOBA_TILE), lambda p, t: (p, t)),
                  pl.BlockSpec((LANES, MOBA_TILE), lambda p, t: (p, jnp.minimum(t + 1, last))),
                  pl.BlockSpec((s, LANES), lambda p, t: (0, NAT_KA + p)),
                  pl.BlockSpec((LANES, s), lambda p, t: (p, 0)),
                  pl.BlockSpec((n_blocks, LANES), lambda p, t: (0, p))],
        out_specs=pl.BlockSpec((MOBA_TILE, LANES), lambda p, t: (t, p)),
        out_shape=jax.ShapeDtypeStruct((s, 2 * LANES), bf16),
        scratch_shapes=_flash_scratch(2, HEAD_DIM, MOBA_TILE) + [pltpu.VMEM((2 * MOBA_TOPK, MOBA_TILE), f32)],
        compiler_params=_params(2),
        name="moba_attention",
    )(q_t, q_t, nat, v_t, kmean)


def _swa_kernel(sink_ref, q_ref, k_ref, kp_ref, v_ref, vp_ref, o_ref):
    p_id = pl.program_id(0)
    t = pl.program_id(1)
    sub_tiles = SWA_TILE // SWA_WINDOW
    lane = lax.broadcasted_iota(jnp.int32, (1, LANES), 1)
    zero = jnp.zeros((SWA_WINDOW, LANES), bf16)
    qi = lax.broadcasted_iota(jnp.int32, (SWA_WINDOW, 2 * SWA_WINDOW), 0)
    kj = lax.broadcasted_iota(jnp.int32, (SWA_WINDOW, 2 * SWA_WINDOW), 1)
    back = qi + SWA_WINDOW - kj
    in_window = (back >= 0) & (back < SWA_WINDOW)
    for sub in range(sub_tiles):
        rows = slice(sub * SWA_WINDOW, (sub + 1) * SWA_WINDOW)
        if sub == 0:
            k_prev, v_prev = kp_ref[...], vp_ref[...]
            mask = in_window & (kj >= SWA_WINDOW - t * SWA_TILE)
        else:
            prev = slice((sub - 1) * SWA_WINDOW, sub * SWA_WINDOW)
            k_prev, v_prev = k_ref[prev, :], v_ref[prev, :]
            mask = in_window
        k_band = jnp.concatenate([k_prev, k_ref[rows, :]], axis=0)
        v_band = jnp.concatenate([v_prev, v_ref[rows, :]], axis=0)
        q = q_ref[rows, :]
        outs = []
        for h, qh in enumerate((jnp.where(lane < HEAD_DIM, q, zero), jnp.where(lane < HEAD_DIM, zero, q))):
            sink = sink_ref[2 * p_id + h]
            s = lax.dot_general(qh, k_band, _NT, preferred_element_type=f32)
            s = jnp.where(mask, s, NEG)
            m = jnp.maximum(jnp.max(s, axis=-1, keepdims=True), sink)
            p = jnp.exp(s - m)
            denom = jnp.sum(p, axis=-1, keepdims=True) + jnp.exp(sink - m)
            outs.append(jnp.dot(p.astype(bf16), v_band, preferred_element_type=f32) / denom)
        o_ref[rows, :] = jnp.where(lane < HEAD_DIM, outs[0], outs[1]).astype(bf16)


def _swa(nat, sinks):
    s = nat.shape[0]
    ratio = SWA_TILE // SWA_WINDOW

    def prev_rows(t):
        return jnp.maximum(t * ratio - 1, 0)

    return pl.pallas_call(
        _swa_kernel,
        grid=(2, s // SWA_TILE),
        in_specs=[pl.BlockSpec(memory_space=pltpu.SMEM),
                  pl.BlockSpec((SWA_TILE, LANES), lambda p, t: (t, NAT_QB + p)),
                  pl.BlockSpec((SWA_TILE, LANES), lambda p, t: (t, NAT_KB + p)),
                  pl.BlockSpec((SWA_WINDOW, LANES), lambda p, t: (prev_rows(t), NAT_KB + p)),
                  pl.BlockSpec((SWA_TILE, LANES), lambda p, t: (t, NAT_VB + p)),
                  pl.BlockSpec((SWA_WINDOW, LANES), lambda p, t: (prev_rows(t), NAT_VB + p))],
        out_specs=pl.BlockSpec((SWA_TILE, LANES), lambda p, t: (t, p)),
        out_shape=jax.ShapeDtypeStruct((s, 2 * LANES), bf16),
        compiler_params=_params(2),
        name="swa_attention",
    )(sinks, nat, nat, nat, nat, nat)


def _diff_kernel(lam_init, qt_ref, qn_ref, k_ref, vt_ref, lq1_ref, lk1_ref, lq2_ref, lk2_ref, sub_ref,
                 o_ref, m_ref, acc_ref, sa_ref, mxa_ref, sb_ref, mxb_ref):
    t = pl.program_id(1)
    t_next = jnp.minimum(t + 1, pl.num_programs(1) - 1)

    def block_rows(g, b):
        return pl.ds(pl.multiple_of((g * GROUP_BLOCKS + b) * MOBA_BLOCK, MOBA_BLOCK), MOBA_BLOCK)

    def tile_scores(tile, q_ref):
        q_maps = _split_rows(q_ref[...])

        def past_score(g, b, h):
            return jnp.dot(k_ref[block_rows(g, b), :], q_maps[h], preferred_element_type=f32)

        def own_score(b, h):
            group = tile * DIFF_TILE // KEY_GROUP
            key, query = _positions((group * GROUP_BLOCKS + b) * MOBA_BLOCK, MOBA_BLOCK, tile, DIFF_TILE)
            return jnp.where(key <= query, past_score(group, b, h), NEG)

        return own_score, past_score

    def value(g, b, h):
        return vt_ref[:, block_rows(g, b)]

    own_score, past_score = tile_scores(t, qt_ref)
    _flash_pipeline(2, t == 0, t * DIFF_TILE // KEY_GROUP, own_score, past_score,
                    lambda: tile_scores(t_next, qn_ref)[0], value,
                    (_Slot(sa_ref, mxa_ref), _Slot(sb_ref, mxb_ref)), (m_ref, acc_ref))

    lam = (jnp.exp(jnp.sum(lq1_ref[...] * lk1_ref[...], axis=-1, keepdims=True))
           - jnp.exp(jnp.sum(lq2_ref[...] * lk2_ref[...], axis=-1, keepdims=True)) + lam_init)
    maps = [acc_ref[h, :LANES, :] * (1.0 / acc_ref[h, LANES:LANES + 1, :]) for h in range(2)]
    out_t = maps[0] - lam * maps[1]
    inv = lax.rsqrt(jnp.mean(out_t * out_t, axis=0, keepdims=True) + EPS) * (1.0 - lam_init)
    o_ref[...] = ((out_t * inv).T * sub_ref[...]).astype(bf16)


def _diff(nat, q_t, v_t, lq1, lk1, lq2, lk2, subln, lam_init):
    s = nat.shape[0]
    vec = _const_spec((1, HEAD_DIM))
    last = s // DIFF_TILE - 1
    return pl.pallas_call(
        functools.partial(_diff_kernel, lam_init),
        grid=(4, s // DIFF_TILE),
        in_specs=[pl.BlockSpec((LANES, DIFF_TILE), lambda h, t: (2 + h, t)),
                  pl.BlockSpec((LANES, DIFF_TILE), lambda h, t: (2 + h, jnp.minimum(t + 1, last))),
                  pl.BlockSpec((s, LANES), lambda h, t: (0, NAT_KC + h)),
                  pl.BlockSpec((LANES, s), lambda h, t: (2 + h, 0)),
                  vec, vec, vec, vec, _const_spec((1, LANES))],
        out_specs=pl.BlockSpec((DIFF_TILE, LANES), lambda h, t: (t, h)),
        out_shape=jax.ShapeDtypeStruct((s, 4 * LANES), bf16),
        scratch_shapes=_flash_scratch(2, LANES, DIFF_TILE),
        compiler_params=_params(2),
        name="diff_attention",
    )(q_t, q_t, nat, v_t, lq1, lk1, lq2, lk2, subln)


def _merge_kernel(x_ref, g_ref, wgate_ref, oa_ref, ob_ref, oc_ref, pa_ref, pb_ref, pc_ref, wo_ref, o_ref):
    x = x_ref[...]
    h = _rms(x, g_ref[...]).astype(bf16)
    branches = ((oa_ref, pa_ref), (ob_ref, pb_ref), (oc_ref, pc_ref))
    merged = jnp.zeros_like(x)
    for i, (b_ref, p_ref) in enumerate(branches):
        logits = jnp.dot(h, wgate_ref[:, i * D_MODEL:(i + 1) * D_MODEL], preferred_element_type=f32)
        y = jnp.dot(b_ref[...], p_ref[...], preferred_element_type=f32)
        merged = merged + jax.nn.sigmoid(logits) * y
    o_ref[...] = x + jnp.dot(merged.astype(bf16), wo_ref[...], preferred_element_type=f32)


def _merge(x, gain, w_gate, oa, ob, oc, pa, pb, pc, w_out, layer):
    s = x.shape[0]

    def rows(width):
        return pl.BlockSpec((MERGE_TILE, width), lambda i: (i, 0))

    return pl.pallas_call(
        _merge_kernel,
        grid=(s // MERGE_TILE,),
        in_specs=[rows(D_MODEL), _const_spec((1, D_MODEL)), _const_spec((D_MODEL, GATE_COLS)),
                  rows(oa.shape[1]), rows(ob.shape[1]), rows(oc.shape[1]),
                  _layer_spec(pa.shape[1:], layer), _layer_spec(pb.shape[1:], layer),
                  _layer_spec(pc.shape[1:], layer), _layer_spec((D_MODEL, D_MODEL), layer)],
        out_specs=rows(D_MODEL),
        out_shape=jax.ShapeDtypeStruct(x.shape, f32),
        compiler_params=_params(1),
        name="gated_merge",
    )(x, gain, w_gate, oa, ob, oc, pa, pb, pc, w_out)


def _rope_tables(seq):
    pos = jnp.arange(seq, dtype=f32)
    inv_freq = ROPE_THETA ** (-jnp.arange(0, HEAD_DIM, 2, dtype=f32) / HEAD_DIM)
    ang = pos[:, None] * inv_freq[None, :]
    cos, sin = jnp.cos(ang), jnp.sin(ang)
    return (jnp.tile(cos, (1, 4)), jnp.concatenate([-sin, sin, -sin, sin], axis=-1), cos.T, sin.T)


def _layer_weights(w_in):
    qa, ka, va = w_in[:, 0:256], w_in[:, 256:512], w_in[:, 512:768]
    qb, kb, vb = w_in[:, 768:1024], w_in[:, 1024:1152], w_in[:, 1152:1280]
    qc, kc, vc = w_in[:, 1280:1792], w_in[:, 1792:2304], w_in[:, 2304:2816]
    gates = w_in[:, 2816:]

    def dup(w):
        h0, h1 = w[:, :HEAD_DIM], w[:, HEAD_DIM:]
        return jnp.concatenate([h0, h0, h1, h1], axis=1)

    w_nat = jnp.concatenate([ka, kc, qb, dup(kb), dup(vb)], axis=1)
    w_t = jnp.concatenate([qa, qc, va, vc], axis=1).T
    return w_nat, w_t, gates


def _head_gains(qa, ka, qb, kb, qc, kc):
    scale = HEAD_DIM ** -0.5
    scale2 = scale * math.log2(math.e)
    nat = jnp.concatenate([jnp.tile(ka, 4), jnp.tile(kc, 8), jnp.tile(qb * scale, 4), jnp.tile(kb, 4)])
    feat = jnp.concatenate([jnp.tile(qa * scale2, 4), jnp.tile(qc * scale2, 8)])
    return nat[None, :], feat[:, None]


def kernel(x, ffn1_norm, ffn1_w_gate, ffn1_w_up, ffn1_w_down, mix_norm, w_in, moba_q_norm, moba_k_norm, swa_q_norm, swa_k_norm, swa_sinks, diff_q_norm, diff_k_norm, diff_lambda_q1, diff_lambda_k1, diff_lambda_q2, diff_lambda_k2, diff_subln, w_branch_a, w_branch_b, w_branch_c, w_out, ffn2_norm, ffn2_w_gate, ffn2_w_up, ffn2_w_down):
    batch, seq, _ = x.shape
    assert batch == 1 and seq % SWA_TILE == 0 and seq % ROW_TILE == 0 and seq % KEY_GROUP == 0
    depth = w_in.shape[0]
    cos, sin, cos_t, sin_t = _rope_tables(seq)
    ffn1 = [w.astype(bf16) for w in (ffn1_w_gate, ffn1_w_up, ffn1_w_down)]
    ffn2 = [w.astype(bf16) for w in (ffn2_w_gate, ffn2_w_up, ffn2_w_down)]
    out_w = [w.astype(bf16) for w in (w_branch_a, w_branch_b, w_branch_c, w_out)]
    w_in = w_in.astype(bf16)
    xs = x[0]
    for l in range(depth):
        lam_init = 0.8 - 0.6 * math.exp(-0.3 * l)
        xs = _ffn(xs, ffn1_norm[l][None], *ffn1, l)
        w_nat, w_t, w_gate = _layer_weights(w_in[l])
        hg_nat, hg_t = _head_gains(moba_q_norm[l], moba_k_norm[l], swa_q_norm[l], swa_k_norm[l],
                                   diff_q_norm[l], diff_k_norm[l])
        nat, q_t, v_t, kmean = _proj(xs, mix_norm[l][None], w_nat, w_t, hg_nat, hg_t, cos, sin, cos_t, sin_t)
        oa = _moba(nat, q_t, v_t, kmean.reshape(seq // MOBA_BLOCK, 2 * LANES))
        ob = _swa(nat, swa_sinks[l])
        oc = _diff(nat, q_t, v_t, diff_lambda_q1[l][None], diff_lambda_k1[l][None], diff_lambda_q2[l][None],
                   diff_lambda_k2[l][None], diff_subln[l][None], lam_init)
        xs = _merge(xs, mix_norm[l][None], w_gate, oa, ob, oc, *out_w, l)
        xs = _ffn(xs, ffn2_norm[l][None], *ffn2, l)
    return xs[None]
```
